```python
import jax, jax.numpy as jnp
from jax import lax
import numpy as np

D_MODEL = 1024
BATCH = 4
SEQ = 8192
DEPTH = 1

CHUNK = 64
Q_BLOCK = 128
FOX_HEAD_DIM = 128
N_FOX_HEADS = D_MODEL // FOX_HEAD_DIM
FOX_WIDTH = N_FOX_HEADS * FOX_HEAD_DIM
SGU_GROUP_DIM = 128
N_SGU_GROUPS = D_MODEL // SGU_GROUP_DIM
SGU_WIDTH = N_SGU_GROUPS * SGU_GROUP_DIM
SGU_LEN = 128
D_FF = 4 * D_MODEL
RMS_EPS = 1e-6
LN_EPS = 1e-5

COL_Q = 0
COL_K = COL_Q + FOX_WIDTH
COL_V = COL_K + FOX_WIDTH
COL_F = COL_V + FOX_WIDTH
COL_U = COL_F + N_FOX_HEADS
COL_SV = COL_U + SGU_WIDTH
COL_GA = COL_SV + SGU_WIDTH
COL_GB = COL_GA + D_MODEL
IN_WIDTH = COL_GB + D_MODEL

kernel_name = "fox_gmlp_gated_macaron_block"


def rmsnorm(x, g):
    xf = x.astype(jnp.float32)
    y = xf * lax.rsqrt(jnp.mean(xf * xf, axis=-1, keepdims=True) + RMS_EPS)
    return (y * g.astype(jnp.float32)).astype(x.dtype)


def swiglu(h, w_gate, w_up, w_down):
    return (jax.nn.silu(h @ w_gate) * (h @ w_up)) @ w_down


def forgetting_attention(q, k, v, f_logit, b_forget):
    B, S, H, D = q.shape
    nb = S // Q_BLOCK
    scale = 1.0 / np.sqrt(D).astype(np.float32)
    log_f = jax.nn.log_sigmoid(f_logit.astype(jnp.float32) + b_forget.astype(jnp.float32))
    c = jnp.cumsum(log_f, axis=1).transpose(0, 2, 1)
    kh = k.transpose(0, 2, 1, 3)
    vh = v.transpose(0, 2, 1, 3)
    qb = q.transpose(0, 2, 1, 3).reshape(B, H, nb, Q_BLOCK, D).transpose(2, 0, 1, 3, 4)
    cb = c.reshape(B, H, nb, Q_BLOCK).transpose(2, 0, 1, 3)
    k_pos = jnp.arange(S)

    def block(args):
        q_blk, c_blk, idx = args
        s = jnp.einsum('bhqd,bhkd->bhqk', q_blk, kh).astype(jnp.float32) * scale
        s = s + c_blk[..., :, None] - c[:, :, None, :]
        q_pos = idx * Q_BLOCK + jnp.arange(Q_BLOCK)
        s = jnp.where(k_pos[None, :] <= q_pos[:, None], s, -jnp.inf)
        p = jax.nn.softmax(s, axis=-1).astype(vh.dtype)
        return jnp.einsum('bhqk,bhkd->bhqd', p, vh)

    out = lax.map(block, (qb, cb, jnp.arange(nb)))
    return out.transpose(1, 0, 3, 2, 4).reshape(B, S, H * D)


def spatial_gating(u, v, ln_g, ln_b, w_s, b_s):
    B, S, W = v.shape
    G, C, L = N_SGU_GROUPS, SGU_GROUP_DIM, SGU_LEN
    vf = v.astype(jnp.float32).reshape(B, S, G, C)
    mu = jnp.mean(vf, axis=-1, keepdims=True)
    var = jnp.mean(jnp.square(vf - mu), axis=-1, keepdims=True)
    vn = ((vf - mu) * lax.rsqrt(var + LN_EPS)).reshape(B, S, W)
    vn = (vn * ln_g.astype(jnp.float32) + ln_b.astype(jnp.float32)).reshape(B, S // L, L, G, C)
    pos = jnp.arange(L)
    mask = (pos[None, :] // CHUNK) <= (pos[:, None] // CHUNK)
    w = jnp.where(mask[None], w_s.astype(jnp.float32), 0.0)
    mixed = jnp.einsum('gts,bnsgc->bntgc', w, vn) + b_s.astype(jnp.float32).T[None, None, :, :, None]
    return u * mixed.reshape(B, S, W).astype(u.dtype)


def setup_inputs(seed: int = 0) -> dict:
    key = jax.random.key(seed)
    ks = jax.random.split(key, 24)
    L, D, F = DEPTH, D_MODEL, D_FF
    nrm = lambda k, shape, fan_in: jax.random.normal(k, shape, jnp.float32) * (fan_in ** -0.5)
    gain = lambda k, shape: 1.0 + 0.05 * jax.random.normal(k, shape, jnp.float32)
    return {
        "x": jax.random.normal(ks[0], (BATCH, SEQ, D), jnp.float32),
        "ffn1_pre_g": gain(ks[1], (L, D)),
        "ffn1_w_gate": nrm(ks[2], (L, D, F), D),
        "ffn1_w_up": nrm(ks[3], (L, D, F), D),
        "ffn1_w_down": nrm(ks[4], (L, F, D), F),
        "ffn1_post_g": gain(ks[5], (L, D)),
        "mix_pre_g": gain(ks[6], (L, D)),
        "w_in": nrm(ks[7], (L, D, IN_WIDTH), D),
        "b_forget": jax.random.uniform(ks[8], (L, N_FOX_HEADS), jnp.float32, 2.0, 6.0),
        "sgu_ln_g": gain(ks[9], (L, SGU_WIDTH)),
        "sgu_ln_b": 0.02 * jax.random.normal(ks[10], (L, SGU_WIDTH), jnp.float32),
        "sgu_w_s": nrm(ks[11], (L, N_SGU_GROUPS, SGU_LEN, SGU_LEN), SGU_LEN),
        "sgu_b_s": 1.0 + 0.02 * jax.random.normal(ks[12], (L, N_SGU_GROUPS, SGU_LEN), jnp.float32),
        "w_out": nrm(ks[13], (L, D, D), D),
        "mix_post_g": gain(ks[14], (L, D)),
        "ffn2_pre_g": gain(ks[15], (L, D)),
        "ffn2_w_gate": nrm(ks[16], (L, D, F), D),
        "ffn2_w_up": nrm(ks[17], (L, D, F), D),
        "ffn2_w_down": nrm(ks[18], (L, F, D), F),
        "ffn2_post_g": gain(ks[19], (L, D)),
    }


def reference(x, ffn1_pre_g, ffn1_w_gate, ffn1_w_up, ffn1_w_down, ffn1_post_g,
              mix_pre_g, w_in, b_forget, sgu_ln_g, sgu_ln_b, sgu_w_s, sgu_b_s,
              w_out, mix_post_g, ffn2_pre_g, ffn2_w_gate, ffn2_w_up, ffn2_w_down,
              ffn2_post_g):
    B, S, D = x.shape
    H, HD = N_FOX_HEADS, FOX_HEAD_DIM
    for l in range(DEPTH):
        h = rmsnorm(x, ffn1_pre_g[l])
        x = x + 0.5 * rmsnorm(swiglu(h, ffn1_w_gate[l], ffn1_w_up[l], ffn1_w_down[l]), ffn1_post_g[l])

        h = rmsnorm(x, mix_pre_g[l])
        z = h @ w_in[l]
        q = z[..., COL_Q:COL_K].reshape(B, S, H, HD)
        k = z[..., COL_K:COL_V].reshape(B, S, H, HD)
        v = z[..., COL_V:COL_F].reshape(B, S, H, HD)
        f_logit = z[..., COL_F:COL_U]
        u_s = jax.nn.gelu(z[..., COL_U:COL_SV], approximate=False)
        v_s = jax.nn.gelu(z[..., COL_SV:COL_GA], approximate=False)
        gate_a = jax.nn.sigmoid(z[..., COL_GA:COL_GB])
        gate_b = jax.nn.sigmoid(z[..., COL_GB:IN_WIDTH])

        o_a = forgetting_attention(q, k, v, f_logit, b_forget[l])
        o_b = spatial_gating(u_s, v_s, sgu_ln_g[l], sgu_ln_b[l], sgu_w_s[l], sgu_b_s[l])
        merged = gate_a * o_a + gate_b * o_b
        x = x + rmsnorm(merged @ w_out[l], mix_post_g[l])

        h = rmsnorm(x, ffn2_pre_g[l])
        x = x + 0.5 * rmsnorm(swiglu(h, ffn2_w_gate[l], ffn2_w_up[l], ffn2_w_down[l]), ffn2_post_g[l])
    return x
```

```python
import functools
import math

import jax
import jax.numpy as jnp
from jax import lax
from jax.experimental import pallas as pl
from jax.experimental.pallas import tpu as pltpu

D_MODEL = 1024
D_FF = 4 * D_MODEL
HEAD_DIM = 128
N_HEADS = D_MODEL // HEAD_DIM
GROUP_DIM = 128
SGU_LEN = 128
CHUNK = 64
RMS_EPS = 1e-6
LN_EPS = 1e-5

LANES = 128
HEADS_PER_STEP = 2
N_SECTIONS = 7
CHUNK_COLS = N_SECTIONS * HEADS_PER_STEP * HEAD_DIM
QK_WIDTH = 2 * HEAD_DIM
C_PARTS = 3
Q_PART_BASE = 0
K_PART_BASE = 32
CUMSUM_ROWS = 256
MASK_VALUE = -1e30
VMEM_LIMIT_BYTES = 56 * 1024 * 1024

F32 = jnp.float32
BF16 = jnp.bfloat16


def _rms(x, g):
    return x * lax.rsqrt(jnp.mean(x * x, axis=-1, keepdims=True) + RMS_EPS) * g


def _dot(a, b):
    return jnp.dot(a, b, preferred_element_type=F32)


def _gelu(x):
    return 0.5 * x * (1.0 + lax.erf(x * math.sqrt(0.5)))


def _ffn_kernel(*refs, n_f, has_mix):
    if has_mix:
        (m_ref, wo_ref, mpost_ref, x_ref, pre_ref, wg_ref, wu_ref, wd_ref, post_ref,
         o_ref, h_ref, acc_ref, xres_ref) = refs
    else:
        (x_ref, pre_ref, wg_ref, wu_ref, wd_ref, post_ref, o_ref, h_ref, acc_ref) = refs
    f = pl.program_id(1)

    @pl.when(f == 0)
    def _():
        x = x_ref[...]
        if has_mix:
            x = x + _rms(_dot(m_ref[...], wo_ref[...]), mpost_ref[...])
            xres_ref[...] = x
        h_ref[...] = _rms(x, pre_ref[...]).astype(BF16)
        acc_ref[...] = jnp.zeros_like(acc_ref)

    h = h_ref[...]
    g = _dot(h, wg_ref[...])
    u = _dot(h, wu_ref[...])
    a = (g * jax.nn.sigmoid(g) * u).astype(BF16)
    acc_ref[...] += _dot(a, wd_ref[...])

    @pl.when(f == n_f - 1)
    def _():
        x = xres_ref[...] if has_mix else x_ref[...]
        o_ref[...] = x + 0.5 * _rms(acc_ref[...], post_ref[...])


def _ffn(x, pre_g, wg, wu, wd, post_g, mix=None, *, tm, tf):
    n, d = x.shape
    n_f = D_FF // tf
    row = lambda i, f: (i, 0)
    const = lambda i, f: (0, 0)
    in_specs = [
        pl.BlockSpec((tm, d), row),
        pl.BlockSpec((1, d), const),
        pl.BlockSpec((d, tf), lambda i, f: (0, f)),
        pl.BlockSpec((d, tf), lambda i, f: (0, f)),
        pl.BlockSpec((tf, d), lambda i, f: (f, 0)),
        pl.BlockSpec((1, d), const),
    ]
    args = [x, pre_g, wg, wu, wd, post_g]
    scratch = [pltpu.VMEM((tm, d), BF16), pltpu.VMEM((tm, d), F32)]
    if mix is not None:
        merged, w_out, mix_post_g = mix
        in_specs = [pl.BlockSpec((tm, d), row), pl.BlockSpec((d, d), const),
                    pl.BlockSpec((1, d), const)] + in_specs
        args = [merged, w_out, mix_post_g] + args
        scratch.append(pltpu.VMEM((tm, d), F32))
    return pl.pallas_call(
        functools.partial(_ffn_kernel, n_f=n_f, has_mix=mix is not None),
        grid=(n // tm, n_f),
        in_specs=in_specs,
        out_specs=pl.BlockSpec((tm, d), row),
        out_shape=jax.ShapeDtypeStruct((n, d), F32),
        scratch_shapes=scratch,
        compiler_params=pltpu.CompilerParams(
            dimension_semantics=("parallel", "arbitrary"), vmem_limit_bytes=VMEM_LIMIT_BYTES),
        name="ffn_mix" if mix is not None else "ffn",
    )(*args)


def _cumsum_rows(lf, carry):
    rows = lf.shape[0]
    r = lax.broadcasted_iota(jnp.int32, (CUMSUM_ROWS, CUMSUM_ROWS), 0)
    c = lax.broadcasted_iota(jnp.int32, (CUMSUM_ROWS, CUMSUM_ROWS), 1)
    tri = (c <= r).astype(BF16)
    out = []
    for b in range(rows // CUMSUM_ROWS):
        blk = lf[b * CUMSUM_ROWS:(b + 1) * CUMSUM_ROWS]
        t1 = blk.astype(BF16)
        r1 = blk - t1.astype(F32)
        t2 = r1.astype(BF16)
        t3 = (r1 - t2.astype(F32)).astype(BF16)
        cs = (_dot(tri, t1) + _dot(tri, t2)) + _dot(tri, t3) + carry
        carry = cs[CUMSUM_ROWS - 1:CUMSUM_ROWS, :]
        out.append(cs)
    return jnp.concatenate(out, axis=0), carry


def _proj_kernel(x_ref, pre_ref, wmain_ref, wf_ref, bf_ref, lng_ref, lnb_ref, ws_ref, bs_ref,
                 q_ref, k_ref, v_ref, ga_ref, gob_ref,
                 h_ref, cpart_ref, carry_ref, *, tm, blocks_per_seq):
    i = pl.program_id(0)
    c = pl.program_id(1)
    lane = lax.broadcasted_iota(jnp.int32, (tm, LANES), 1)
    in_q_parts = lane < Q_PART_BASE + C_PARTS * N_HEADS
    in_k_parts = (lane >= K_PART_BASE) & (lane < K_PART_BASE + C_PARTS * N_HEADS)

    @pl.when(c == 0)
    def _():
        h = _rms(x_ref[...], pre_ref[...]).astype(BF16)
        h_ref[...] = h

        @pl.when(i % blocks_per_seq == 0)
        def _():
            carry_ref[...] = jnp.zeros_like(carry_ref)

        lf = jax.nn.log_sigmoid(_dot(h, wf_ref[...]) + bf_ref[...])
        lf = jnp.where(in_q_parts | in_k_parts, lf, 0.0)
        cs, carry = _cumsum_rows(lf, carry_ref[0:1, :])
        carry_ref[0:1, :] = carry
        hi = cs.astype(BF16).astype(F32)
        r1 = cs - hi
        mid = r1.astype(BF16).astype(F32)
        lo = r1 - mid
        part = (lane % 32) // N_HEADS
        cpart_ref[...] = jnp.where(part == 0, hi, jnp.where(part == 1, mid, lo))

    z = _dot(h_ref[...], wmain_ref[...])
    sec = HEADS_PER_STEP * HEAD_DIM

    def section(s, j):
        return z[:, s * sec + j * HEAD_DIM: s * sec + (j + 1) * HEAD_DIM]

    cpart = cpart_ref[...]
    scale = 1.0 / math.sqrt(HEAD_DIM)
    wr = lax.broadcasted_iota(jnp.int32, (SGU_LEN, SGU_LEN), 0)
    wc = lax.broadcasted_iota(jnp.int32, (SGU_LEN, SGU_LEN), 1)
    chunk_causal = (wc // CHUNK) <= (wr // CHUNK)
    for j in range(HEADS_PER_STEP):
        head = c * HEADS_PER_STEP + j
        is_head = (lane % N_HEADS) == head
        q_aug = jnp.where(in_q_parts, cpart, jnp.where(in_k_parts & is_head, 1.0, 0.0))
        k_aug = jnp.where(in_k_parts, -cpart, jnp.where(in_q_parts & is_head, 1.0, 0.0))
        q_ref[0, j, :, 0:HEAD_DIM] = (section(0, j) * scale).astype(BF16)
        q_ref[0, j, :, HEAD_DIM:QK_WIDTH] = q_aug.astype(BF16)
        k_ref[0, j, :, 0:HEAD_DIM] = section(1, j).astype(BF16)
        k_ref[0, j, :, HEAD_DIM:QK_WIDTH] = k_aug.astype(BF16)
        v_ref[0, j] = section(2, j).astype(BF16)

        cols = slice(j * GROUP_DIM, (j + 1) * GROUP_DIM)
        u = _gelu(section(3, j))
        sv = _gelu(section(4, j))
        mu = jnp.mean(sv, axis=-1, keepdims=True)
        dv = sv - mu
        var = jnp.mean(dv * dv, axis=-1, keepdims=True)
        vn = (dv * lax.rsqrt(var + LN_EPS) * lng_ref[:, cols] + lnb_ref[:, cols]).astype(BF16)
        w = jnp.where(chunk_causal, ws_ref[j], 0.0).astype(BF16)
        b = bs_ref[j]
        mixed = [
            _dot(w, vn[n * SGU_LEN:(n + 1) * SGU_LEN, :]) + b for n in range(tm // SGU_LEN)
        ]
        mixed = jnp.concatenate(mixed, axis=0)
        ga_ref[:, cols] = jax.nn.sigmoid(section(5, j)).astype(BF16)
        gob_ref[:, cols] = (jax.nn.sigmoid(section(6, j)) * u * mixed).astype(BF16)


def _proj(x1, pre_g, wmain, wf, bf, ln_g, ln_b, w_s, b_s, *, batch, seq, tm):
    n, d = x1.shape
    n_c = N_HEADS // HEADS_PER_STEP
    bps = seq // tm
    hp = HEADS_PER_STEP
    head_blk = lambda i, c: (i // bps, c, i % bps, 0)
    qk_shape = jax.ShapeDtypeStruct((batch, N_HEADS, seq, QK_WIDTH), BF16)
    return pl.pallas_call(
        functools.partial(_proj_kernel, tm=tm, blocks_per_seq=bps),
        grid=(n // tm, n_c),
        in_specs=[
            pl.BlockSpec((tm, d), lambda i, c: (i, 0)),
            pl.BlockSpec((1, d), lambda i, c: (0, 0)),
            pl.BlockSpec((d, CHUNK_COLS), lambda i, c: (0, c)),
            pl.BlockSpec((d, LANES), lambda i, c: (0, 0)),
            pl.BlockSpec((1, LANES), lambda i, c: (0, 0)),
            pl.BlockSpec((1, hp * GROUP_DIM), lambda i, c: (0, c)),
            pl.BlockSpec((1, hp * GROUP_DIM), lambda i, c: (0, c)),
            pl.BlockSpec((hp, SGU_LEN, SGU_LEN), lambda i, c: (c, 0, 0)),
            pl.BlockSpec((hp, SGU_LEN, 1), lambda i, c: (c, 0, 0)),
        ],
        out_specs=[
            pl.BlockSpec((1, hp, tm, QK_WIDTH), head_blk),
            pl.BlockSpec((1, hp, tm, QK_WIDTH), head_blk),
            pl.BlockSpec((1, hp, tm, HEAD_DIM), head_blk),
            pl.BlockSpec((tm, hp * GROUP_DIM), lambda i, c: (i, c)),
            pl.BlockSpec((tm, hp * GROUP_DIM), lambda i, c: (i, c)),
        ],
        out_shape=[
            qk_shape, qk_shape,
            jax.ShapeDtypeStruct((batch, N_HEADS, seq, HEAD_DIM), BF16),
            jax.ShapeDtypeStruct((n, d), BF16),
            jax.ShapeDtypeStruct((n, d), BF16),
        ],
        scratch_shapes=[pltpu.VMEM((tm, d), BF16), pltpu.VMEM((tm, LANES), F32),
                        pltpu.VMEM((8, LANES), F32)],
        compiler_params=pltpu.CompilerParams(
            dimension_semantics=("arbitrary", "arbitrary"), vmem_limit_bytes=VMEM_LIMIT_BYTES),
        name="proj",
    )(x1, pre_g, wmain, wf, bf, ln_g, ln_b, w_s, b_s)


def _attn_kernel(q_ref, k_ref, v_ref, ga_ref, gob_ref, o_ref, *, tq, tk):
    i = pl.program_id(2)
    q = q_ref[0, 0]

    def step(start, carry, masked):
        m, l, acc = carry
        kb = k_ref[0, 0, pl.ds(start, tk), :]
        vb = v_ref[0, 0, pl.ds(start, tk), :]
        s = lax.dot_general(q, kb, (((1,), (1,)), ((), ())), preferred_element_type=F32)
        if masked:
            row = lax.broadcasted_iota(jnp.int32, (tq, tk), 0)
            col = lax.broadcasted_iota(jnp.int32, (tq, tk), 1)
            s = jnp.where(col <= row, s, MASK_VALUE)
        m_new = jnp.maximum(m, jnp.max(s, axis=-1, keepdims=True))
        alpha = jnp.exp(m - m_new)
        p = jnp.exp(s - m_new)
        l = alpha * l + jnp.sum(p, axis=-1, keepdims=True)
        acc = alpha * acc + _dot(p.astype(BF16), vb)
        return m_new, l, acc

    init = (jnp.full((tq, 1), MASK_VALUE, F32), jnp.zeros((tq, 1), F32),
            jnp.zeros((tq, HEAD_DIM), F32))
    blocks_per_q = tq // tk
    carry = lax.fori_loop(
        0, i * blocks_per_q,
        lambda j, cr: step(pl.multiple_of(j * tk, tk), cr, False), init)
    m, l, acc = step(pl.multiple_of(i * tq, tq), carry, True)
    o = acc / l
    o_ref[...] = (ga_ref[...].astype(F32) * o + gob_ref[...].astype(F32)).astype(BF16)


def _attn(q, k, v, ga, gob, *, tq, tk):
    batch, n_heads, seq, _ = q.shape
    assert tq == tk
    nq = seq // tq
    tok_blk = lambda b, h, i: (b * nq + i, h)
    return pl.pallas_call(
        functools.partial(_attn_kernel, tq=tq, tk=tk),
        grid=(batch, n_heads, nq),
        in_specs=[
            pl.BlockSpec((1, 1, tq, QK_WIDTH), lambda b, h, i: (b, h, i, 0)),
            pl.BlockSpec((1, 1, seq, QK_WIDTH), lambda b, h, i: (b, h, 0, 0)),
            pl.BlockSpec((1, 1, seq, HEAD_DIM), lambda b, h, i: (b, h, 0, 0)),
            pl.BlockSpec((tq, HEAD_DIM), tok_blk),
            pl.BlockSpec((tq, HEAD_DIM), tok_blk),
        ],
        out_specs=pl.BlockSpec((tq, HEAD_DIM), tok_blk),
        out_shape=jax.ShapeDtypeStruct(ga.shape, BF16),
        compiler_params=pltpu.CompilerParams(
            dimension_semantics=("parallel", "parallel", "arbitrary"),
            vmem_limit_bytes=VMEM_LIMIT_BYTES),
        name="attn",
    )(q, k, v, ga, gob)


def _pack_w_in(w_in):
    d = w_in.shape[0]
    col_f = 3 * D_MODEL
    starts = [0, D_MODEL, 2 * D_MODEL, col_f + N_HEADS, col_f + N_HEADS + D_MODEL,
              col_f + N_HEADS + 2 * D_MODEL, col_f + N_HEADS + 3 * D_MODEL]
    n_c = N_HEADS // HEADS_PER_STEP
    secs = [w_in[:, s:s + D_MODEL].reshape(d, n_c, 1, HEADS_PER_STEP * HEAD_DIM) for s in starts]
    wmain = jnp.concatenate(secs, axis=2).reshape(d, n_c * CHUNK_COLS).astype(BF16)
    wf = w_in[:, col_f:col_f + N_HEADS]
    return wmain, _replicate_forget(wf).astype(BF16)


def _replicate_forget(a):
    rows = a.shape[0]
    rep = jnp.tile(a, (1, C_PARTS))
    gap = jnp.zeros((rows, K_PART_BASE - C_PARTS * N_HEADS), a.dtype)
    tail = jnp.zeros((rows, LANES - K_PART_BASE - C_PARTS * N_HEADS), a.dtype)
    return jnp.concatenate([rep, gap, rep, tail], axis=1)


def kernel(x, ffn1_pre_g, ffn1_w_gate, ffn1_w_up, ffn1_w_down, ffn1_post_g, mix_pre_g, w_in, b_forget, sgu_ln_g, sgu_ln_b, sgu_w_s, sgu_b_s, w_out, mix_post_g, ffn2_pre_g, ffn2_w_gate, ffn2_w_up, ffn2_w_down, ffn2_post_g):
    batch, seq, d = x.shape
    depth = ffn1_pre_g.shape[0]
    tm_ffn = min(1024, seq)
    tm_proj = min(1024, seq)
    t_attn = min(512, seq)
    xf = x.reshape(batch * seq, d)
    for l in range(depth):
        row = lambda a: a[l].reshape(1, -1)
        xf = _ffn(xf, row(ffn1_pre_g), ffn1_w_gate[l].astype(BF16), ffn1_w_up[l].astype(BF16),
                  ffn1_w_down[l].astype(BF16), row(ffn1_post_g), tm=tm_ffn, tf=512)
        wmain, wf = _pack_w_in(w_in[l])
        q, k, v, ga, gob = _proj(
            xf, row(mix_pre_g), wmain, wf, _replicate_forget(row(b_forget)),
            row(sgu_ln_g), row(sgu_ln_b), sgu_w_s[l], sgu_b_s[l].reshape(N_HEADS, SGU_LEN, 1),
            batch=batch, seq=seq, tm=tm_proj)
        merged = _attn(q, k, v, ga, gob, tq=t_attn, tk=t_attn)
        xf = _ffn(xf, row(ffn2_pre_g), ffn2_w_gate[l].astype(BF16), ffn2_w_up[l].astype(BF16),
                  ffn2_w_down[l].astype(BF16), row(ffn2_post_g),
                  mix=(merged, w_out[l].astype(BF16), row(mix_post_g)), tm=tm_ffn, tf=512)
    return xf.reshape(batch, seq, d)
```

```python
import functools
import math

import jax
import jax.numpy as jnp
from jax import lax
from jax.experimental import pallas as pl
from jax.experimental.pallas import tpu as pltpu

D_MODEL = 1024
D_FF = 4 * D_MODEL
HEAD_DIM = 128
N_HEADS = D_MODEL // HEAD_DIM
GROUP_DIM = 128
SGU_LEN = 128
CHUNK = 64
RMS_EPS = 1e-6
LN_EPS = 1e-5

LANES = 128
HEADS_PER_STEP = 2
N_SECTIONS = 7
CHUNK_COLS = N_SECTIONS * HEADS_PER_STEP * HEAD_DIM
QK_WIDTH = 2 * HEAD_DIM
C_PARTS = 3
Q_PART_BASE = 0
K_PART_BASE = 32
CUMSUM_ROWS = 256
KV_TILE = 512
Q_TILE = 2 * KV_TILE
MASK_VALUE = -1e30
LOG2E = math.log2(math.e)
VMEM_LIMIT_BYTES = 56 * 1024 * 1024

F32 = jnp.float32
BF16 = jnp.bfloat16


def _rms(x, g):
    return x * lax.rsqrt(jnp.mean(x * x, axis=-1, keepdims=True) + RMS_EPS) * g


def _dot(a, b):
    return jnp.dot(a, b, preferred_element_type=F32)


def _gelu(x):
    return 0.5 * x * (1.0 + lax.erf(x * math.sqrt(0.5)))


def _ffn_kernel(*refs, n_f, has_mix):
    if has_mix:
        (m_ref, wo_ref, mpost_ref, x_ref, pre_ref, wg_ref, wu_ref, wd_ref, post_ref,
         o_ref, h_ref, acc_ref, xres_ref) = refs
    else:
        (x_ref, pre_ref, wg_ref, wu_ref, wd_ref, post_ref, o_ref, h_ref, acc_ref) = refs
    f = pl.program_id(1)

    @pl.when(f == 0)
    def _():
        x = x_ref[...]
        if has_mix:
            x = x + _rms(_dot(m_ref[...], wo_ref[...]), mpost_ref[...])
            xres_ref[...] = x
        h_ref[...] = _rms(x, pre_ref[...]).astype(BF16)
        acc_ref[...] = jnp.zeros_like(acc_ref)

    h = h_ref[...]
    g = _dot(h, wg_ref[...])
    u = _dot(h, wu_ref[...])
    a = (g * jax.nn.sigmoid(g) * u).astype(BF16)
    acc_ref[...] += _dot(a, wd_ref[...])

    @pl.when(f == n_f - 1)
    def _():
        x = xres_ref[...] if has_mix else x_ref[...]
        o_ref[...] = x + 0.5 * _rms(acc_ref[...], post_ref[...])


def _ffn(x, pre_g, wg, wu, wd, post_g, mix=None, *, tm, tf):
    n, d = x.shape
    n_f = D_FF // tf
    row = lambda i, f: (i, 0)
    const = lambda i, f: (0, 0)
    in_specs = [
        pl.BlockSpec((tm, d), row),
        pl.BlockSpec((1, d), const),
        pl.BlockSpec((d, tf), lambda i, f: (0, f)),
        pl.BlockSpec((d, tf), lambda i, f: (0, f)),
        pl.BlockSpec((tf, d), lambda i, f: (f, 0)),
        pl.BlockSpec((1, d), const),
    ]
    args = [x, pre_g, wg, wu, wd, post_g]
    scratch = [pltpu.VMEM((tm, d), BF16), pltpu.VMEM((tm, d), F32)]
    if mix is not None:
        merged, w_out, mix_post_g = mix
        in_specs = [pl.BlockSpec((tm, d), row), pl.BlockSpec((d, d), const),
                    pl.BlockSpec((1, d), const)] + in_specs
        args = [merged, w_out, mix_post_g] + args
        scratch.append(pltpu.VMEM((tm, d), F32))
    return pl.pallas_call(
        functools.partial(_ffn_kernel, n_f=n_f, has_mix=mix is not None),
        grid=(n // tm, n_f),
        in_specs=in_specs,
        out_specs=pl.BlockSpec((tm, d), row),
        out_shape=jax.ShapeDtypeStruct((n, d), F32),
        scratch_shapes=scratch,
        compiler_params=pltpu.CompilerParams(
            dimension_semantics=("parallel", "arbitrary"), vmem_limit_bytes=VMEM_LIMIT_BYTES),
        name="ffn_mix" if mix is not None else "ffn",
    )(*args)


def _cumsum_rows(lf, carry):
    rows = lf.shape[0]
    r = lax.broadcasted_iota(jnp.int32, (CUMSUM_ROWS, CUMSUM_ROWS), 0)
    c = lax.broadcasted_iota(jnp.int32, (CUMSUM_ROWS, CUMSUM_ROWS), 1)
    tri = (c <= r).astype(BF16)
    out = []
    for b in range(rows // CUMSUM_ROWS):
        blk = lf[b * CUMSUM_ROWS:(b + 1) * CUMSUM_ROWS]
        t1 = blk.astype(BF16)
        r1 = blk - t1.astype(F32)
        t2 = r1.astype(BF16)
        t3 = (r1 - t2.astype(F32)).astype(BF16)
        cs = (_dot(tri, t1) + _dot(tri, t2)) + _dot(tri, t3) + carry
        carry = cs[CUMSUM_ROWS - 1:CUMSUM_ROWS, :]
        out.append(cs)
    return jnp.concatenate(out, axis=0), carry


def _proj_kernel(x_ref, pre_ref, wmain_ref, wf_ref, bf_ref, lng_ref, lnb_ref, ws_ref, bs_ref,
                 qt_ref, k_ref, vt_ref, ga_ref, gob_ref,
                 h_ref, cpart_ref, carry_ref, *, tm, blocks_per_seq):
    i = pl.program_id(0)
    c = pl.program_id(1)
    lane = lax.broadcasted_iota(jnp.int32, (tm, LANES), 1)
    in_q_parts = lane < Q_PART_BASE + C_PARTS * N_HEADS
    in_k_parts = (lane >= K_PART_BASE) & (lane < K_PART_BASE + C_PARTS * N_HEADS)

    @pl.when(c == 0)
    def _():
        h = _rms(x_ref[...], pre_ref[...]).astype(BF16)
        h_ref[...] = h

        @pl.when(i % blocks_per_seq == 0)
        def _():
            carry_ref[...] = jnp.zeros_like(carry_ref)

        lf = jax.nn.log_sigmoid(_dot(h, wf_ref[...]) + bf_ref[...])
        lf = jnp.where(in_q_parts | in_k_parts, lf, 0.0)
        cs, carry = _cumsum_rows(lf, carry_ref[0:1, :])
        carry_ref[0:1, :] = carry
        cs = cs * LOG2E
        hi = cs.astype(BF16).astype(F32)
        r1 = cs - hi
        mid = r1.astype(BF16).astype(F32)
        lo = r1 - mid
        part = (lane % 32) // N_HEADS
        cpart_ref[...] = jnp.where(part == 0, hi, jnp.where(part == 1, mid, lo))

    z = _dot(h_ref[...], wmain_ref[...])
    sec = HEADS_PER_STEP * HEAD_DIM

    def section(s, j):
        return z[:, s * sec + j * HEAD_DIM: s * sec + (j + 1) * HEAD_DIM]

    cpart = cpart_ref[...]
    scale = LOG2E / math.sqrt(HEAD_DIM)
    wr = lax.broadcasted_iota(jnp.int32, (SGU_LEN, SGU_LEN), 0)
    wc = lax.broadcasted_iota(jnp.int32, (SGU_LEN, SGU_LEN), 1)
    chunk_causal = (wc // CHUNK) <= (wr // CHUNK)
    for j in range(HEADS_PER_STEP):
        head = c * HEADS_PER_STEP + j
        is_head = (lane % N_HEADS) == head
        q_aug = jnp.where(in_q_parts, cpart, jnp.where(in_k_parts & is_head, 1.0, 0.0))
        k_aug = jnp.where(in_k_parts, -cpart, jnp.where(in_q_parts & is_head, 1.0, 0.0))
        qt_ref[0, j, 0:HEAD_DIM, :] = (section(0, j) * scale).T.astype(BF16)
        qt_ref[0, j, HEAD_DIM:QK_WIDTH, :] = q_aug.T.astype(BF16)
        k_ref[0, j, :, 0:HEAD_DIM] = section(1, j).astype(BF16)
        k_ref[0, j, :, HEAD_DIM:QK_WIDTH] = k_aug.astype(BF16)
        vt = section(2, j).T.astype(BF16)
        for r in range(tm // KV_TILE):
            vt_ref[0, j, r] = vt[:, r * KV_TILE:(r + 1) * KV_TILE]

        cols = slice(j * GROUP_DIM, (j + 1) * GROUP_DIM)
        u = _gelu(section(3, j))
        sv = _gelu(section(4, j))
        mu = jnp.mean(sv, axis=-1, keepdims=True)
        dv = sv - mu
        var = jnp.mean(dv * dv, axis=-1, keepdims=True)
        vn = (dv * lax.rsqrt(var + LN_EPS) * lng_ref[:, cols] + lnb_ref[:, cols]).astype(BF16)
        w = jnp.where(chunk_causal, ws_ref[j], 0.0).astype(BF16)
        b = bs_ref[j]
        mixed = [
            _dot(w, vn[n * SGU_LEN:(n + 1) * SGU_LEN, :]) + b for n in range(tm // SGU_LEN)
        ]
        mixed = jnp.concatenate(mixed, axis=0)
        ga_ref[:, cols] = jax.nn.sigmoid(section(5, j)).astype(BF16)
        gob_ref[:, cols] = (jax.nn.sigmoid(section(6, j)) * u * mixed).astype(BF16)


def _proj(x1, pre_g, wmain, wf, bf, ln_g, ln_b, w_s, b_s, *, batch, seq, tm):
    n, d = x1.shape
    n_c = N_HEADS // HEADS_PER_STEP
    bps = seq // tm
    hp = HEADS_PER_STEP
    return pl.pallas_call(
        functools.partial(_proj_kernel, tm=tm, blocks_per_seq=bps),
        grid=(n // tm, n_c),
        in_specs=[
            pl.BlockSpec((tm, d), lambda i, c: (i, 0)),
            pl.BlockSpec((1, d), lambda i, c: (0, 0)),
            pl.BlockSpec((d, CHUNK_COLS), lambda i, c: (0, c)),
            pl.BlockSpec((d, LANES), lambda i, c: (0, 0)),
            pl.BlockSpec((1, LANES), lambda i, c: (0, 0)),
            pl.BlockSpec((1, hp * GROUP_DIM), lambda i, c: (0, c)),
            pl.BlockSpec((1, hp * GROUP_DIM), lambda i, c: (0, c)),
            pl.BlockSpec((hp, SGU_LEN, SGU_LEN), lambda i, c: (c, 0, 0)),
            pl.BlockSpec((hp, SGU_LEN, 1), lambda i, c: (c, 0, 0)),
        ],
        out_specs=[
            pl.BlockSpec((1, hp, QK_WIDTH, tm), lambda i, c: (i // bps, c, 0, i % bps)),
            pl.BlockSpec((1, hp, tm, QK_WIDTH), lambda i, c: (i // bps, c, i % bps, 0)),
            pl.BlockSpec((1, hp, tm // KV_TILE, HEAD_DIM, KV_TILE),
                         lambda i, c: (i // bps, c, i % bps, 0, 0)),
            pl.BlockSpec((tm, hp * GROUP_DIM), lambda i, c: (i, c)),
            pl.BlockSpec((tm, hp * GROUP_DIM), lambda i, c: (i, c)),
        ],
        out_shape=[
            jax.ShapeDtypeStruct((batch, N_HEADS, QK_WIDTH, seq), BF16),
            jax.ShapeDtypeStruct((batch, N_HEADS, seq, QK_WIDTH), BF16),
            jax.ShapeDtypeStruct((batch, N_HEADS, seq // KV_TILE, HEAD_DIM, KV_TILE), BF16),
            jax.ShapeDtypeStruct((n, d), BF16),
            jax.ShapeDtypeStruct((n, d), BF16),
        ],
        scratch_shapes=[pltpu.VMEM((tm, d), BF16), pltpu.VMEM((tm, LANES), F32),
                        pltpu.VMEM((8, LANES), F32)],
        compiler_params=pltpu.CompilerParams(
            dimension_semantics=("arbitrary", "arbitrary"), vmem_limit_bytes=VMEM_LIMIT_BYTES),
        name="proj",
    )(x1, pre_g, wmain, wf, bf, ln_g, ln_b, w_s, b_s)


def _attn_kernel(qt_ref, k_ref, vt_ref, ga_ref, gob_ref, o_ref,
                 s0_ref, s1_ref, p0_ref, p1_ref, acc_ref):
    i = pl.program_id(2)
    tk, tq, half = KV_TILE, Q_TILE, Q_TILE // 2

    def qk(j, s_ref):
        kb = k_ref[0, 0, pl.ds(pl.multiple_of(j * tk, tk), tk), :]
        s = _dot(kb, qt_ref[0, 0])
        s_ref[...] = s
        return jnp.max(s, axis=0, keepdims=True)

    def softmax(s, s_max, m, l):
        m_new = jnp.maximum(m, s_max)
        alpha = jnp.exp2(m - m_new)
        p = jnp.exp2(s - m_new)
        return m_new, alpha * l + jnp.sum(p, axis=0, keepdims=True), alpha, p.astype(BF16)

    def pv(j, alpha, p):
        acc_ref[...] = alpha * acc_ref[...] + _dot(vt_ref[0, 0, j], p)

    def causal(s):
        kv_pos = lax.broadcasted_iota(jnp.int32, s.shape, 0)
        q_pos = lax.broadcasted_iota(jnp.int32, s.shape, 1)
        return jnp.where(kv_pos <= q_pos, s, MASK_VALUE)

    acc_ref[...] = jnp.zeros_like(acc_ref)
    p1_ref[...] = jnp.zeros_like(p1_ref)
    mx0 = qk(0, s0_ref)

    def trip(t, carry):
        m, l, mx0, a1 = carry
        j = 2 * t
        pv(jnp.maximum(j - 1, 0), a1, p1_ref[...])
        m, l, a0, p = softmax(s0_ref[...], mx0, m, l)
        p0_ref[...] = p
        mx1 = qk(j + 1, s1_ref)
        pv(j, a0, p0_ref[...])
        m, l, a1, p = softmax(s1_ref[...], mx1, m, l)
        p1_ref[...] = p
        mx0 = qk(j + 2, s0_ref)
        return m, l, mx0, a1

    init = (jnp.full((1, tq), MASK_VALUE, F32), jnp.zeros((1, tq), F32), mx0,
            jnp.ones((1, tq), F32))
    m, l, _, a1 = lax.fori_loop(0, i, trip, init)

    d = 2 * i
    pv(jnp.maximum(d - 1, 0), a1, p1_ref[...])
    kb = k_ref[0, 0, pl.ds(pl.multiple_of((d + 1) * tk, tk), tk), :]
    s_hi = causal(_dot(kb, qt_ref[0, 0, :, half:]))
    s_lo = causal(s0_ref[...])
    m, l, a0, p = softmax(s_lo, jnp.max(s_lo, axis=0, keepdims=True), m, l)
    pv(d, a0, p)
    _, l_hi, a1, p_hi = softmax(s_hi, jnp.max(s_hi, axis=0, keepdims=True),
                                m[:, half:], l[:, half:])
    acc_ref[:, half:] = a1 * acc_ref[:, half:] + _dot(vt_ref[0, 0, d + 1], p_hi)
    l = jnp.concatenate([l[:, :half], l_hi], axis=1)
    o = (acc_ref[...] / l).T
    o_ref[...] = (ga_ref[...].astype(F32) * o + gob_ref[...].astype(F32)).astype(BF16)


def _attn(qt, k, vt, ga, gob):
    batch, n_heads, seq, _ = k.shape
    tq = Q_TILE
    nq = seq // tq
    tok_blk = lambda b, h, i: (b * nq + i, h)
    return pl.pallas_call(
        _attn_kernel,
        grid=(batch, n_heads, nq),
        scratch_shapes=[pltpu.VMEM((KV_TILE, tq), F32), pltpu.VMEM((KV_TILE, tq), F32),
                        pltpu.VMEM((KV_TILE, tq), BF16), pltpu.VMEM((KV_TILE, tq), BF16),
                        pltpu.VMEM((HEAD_DIM, tq), F32)],
        in_specs=[
            pl.BlockSpec((1, 1, QK_WIDTH, tq), lambda b, h, i: (b, h, 0, i)),
            pl.BlockSpec((1, 1, seq, QK_WIDTH), lambda b, h, i: (b, h, 0, 0)),
            pl.BlockSpec((1, 1, seq // KV_TILE, HEAD_DIM, KV_TILE), lambda b, h, i: (b, h, 0, 0, 0)),
            pl.BlockSpec((tq, HEAD_DIM), tok_blk),
            pl.BlockSpec((tq, HEAD_DIM), tok_blk),
        ],
        out_specs=pl.BlockSpec((tq, HEAD_DIM), tok_blk),
        out_shape=jax.ShapeDtypeStruct(ga.shape, BF16),
        compiler_params=pltpu.CompilerParams(
            dimension_semantics=("parallel", "parallel", "arbitrary"),
            vmem_limit_bytes=VMEM_LIMIT_BYTES),
        name="attn",
    )(qt, k, vt, ga, gob)


def _pack_w_in(w_in):
    d = w_in.shape[0]
    col_f = 3 * D_MODEL
    starts = [0, D_MODEL, 2 * D_MODEL, col_f + N_HEADS, col_f + N_HEADS + D_MODEL,
              col_f + N_HEADS + 2 * D_MODEL, col_f + N_HEADS + 3 * D_MODEL]
    n_c = N_HEADS // HEADS_PER_STEP
    secs = [w_in[:, s:s + D_MODEL].reshape(d, n_c, 1, HEADS_PER_STEP * HEAD_DIM) for s in starts]
    wmain = jnp.concatenate(secs, axis=2).reshape(d, n_c * CHUNK_COLS).astype(BF16)
    wf = w_in[:, col_f:col_f + N_HEADS]
    return wmain, _replicate_forget(wf).astype(BF16)


def _replicate_forget(a):
    rows = a.shape[0]
    rep = jnp.tile(a, (1, C_PARTS))
    gap = jnp.zeros((rows, K_PART_BASE - C_PARTS * N_HEADS), a.dtype)
    tail = jnp.zeros((rows, LANES - K_PART_BASE - C_PARTS * N_HEADS), a.dtype)
    return jnp.concatenate([rep, gap, rep, tail], axis=1)


def kernel(x, ffn1_pre_g, ffn1_w_gate, ffn1_w_up, ffn1_w_down, ffn1_post_g, mix_pre_g, w_in, b_forget, sgu_ln_g, sgu_ln_b, sgu_w_s, sgu_b_s, w_out, mix_post_g, ffn2_pre_g, ffn2_w_gate, ffn2_w_up, ffn2_w_down, ffn2_post_g):
    batch, seq, d = x.shape
    depth = ffn1_pre_g.shape[0]
    tm_ffn = min(1024, seq)
    tm_proj = min(1024, seq)
    xf = x.reshape(batch * seq, d)
    for l in range(depth):
        row = lambda a: a[l].reshape(1, -1)
        xf = _ffn(xf, row(ffn1_pre_g), ffn1_w_gate[l].astype(BF16), ffn1_w_up[l].astype(BF16),
                  ffn1_w_down[l].astype(BF16), row(ffn1_post_g), tm=tm_ffn, tf=512)
        wmain, wf = _pack_w_in(w_in[l])
        qt, k, vt, ga, gob = _proj(
            xf, row(mix_pre_g), wmain, wf, _replicate_forget(row(b_forget)),
            row(sgu_ln_g), row(sgu_ln_b), sgu_w_s[l], sgu_b_s[l].reshape(N_HEADS, SGU_LEN, 1),
            batch=batch, seq=seq, tm=tm_proj)
        merged = _attn(qt, k, vt, ga, gob)
        xf = _ffn(xf, row(ffn2_pre_g), ffn2_w_gate[l].astype(BF16), ffn2_w_up[l].astype(BF16),
                  ffn2_w_down[l].astype(BF16), row(ffn2_post_g),
                  mix=(merged, w_out[l].astype(BF16), row(mix_post_g)), tm=tm_ffn, tf=512)
    return xf.reshape(batch, seq, d)
```

```python
import functools
import math

import jax
import jax.numpy as jnp
from jax import lax
from jax.experimental import pallas as pl
from jax.experimental.pallas import tpu as pltpu

D_MODEL = 1024
D_FF = 4 * D_MODEL
HEAD_DIM = 128
N_HEADS = D_MODEL // HEAD_DIM
GROUP_DIM = 128
SGU_LEN = 128
CHUNK = 64
RMS_EPS = 1e-6
LN_EPS = 1e-5

LANES = 128
HEADS_PER_STEP = 2
N_SECTIONS = 7
CHUNK_COLS = N_SECTIONS * HEADS_PER_STEP * HEAD_DIM
QK_WIDTH = 2 * HEAD_DIM
C_PARTS = 3
Q_PART_BASE = 0
K_PART_BASE = 32
CUMSUM_ROWS = 256
KV_TILE = 512
Q_TILE = 2 * KV_TILE
MASK_VALUE = -1e30
LOG2E = math.log2(math.e)
SAFE_EXP2_RANGE = 64.0
VMEM_LIMIT_BYTES = 56 * 1024 * 1024

F32 = jnp.float32
BF16 = jnp.bfloat16


def _rms(x, g):
    return x * lax.rsqrt(jnp.mean(x * x, axis=-1, keepdims=True) + RMS_EPS) * g


def _dot(a, b):
    return jnp.dot(a, b, preferred_element_type=F32)


def _gelu(x):
    return 0.5 * x * (1.0 + lax.erf(x * math.sqrt(0.5)))


def _ffn_kernel(*refs, n_f, has_mix):
    if has_mix:
        (m_ref, wo_ref, mpost_ref, x_ref, pre_ref, wg_ref, wu_ref, wd_ref, post_ref,
         o_ref, h_ref, acc_ref, xres_ref) = refs
    else:
        (x_ref, pre_ref, wg_ref, wu_ref, wd_ref, post_ref, o_ref, h_ref, acc_ref) = refs
    f = pl.program_id(1)

    @pl.when(f == 0)
    def _():
        x = x_ref[...]
        if has_mix:
            x = x + _rms(_dot(m_ref[...], wo_ref[...]), mpost_ref[...])
            xres_ref[...] = x
        h_ref[...] = _rms(x, pre_ref[...]).astype(BF16)
        acc_ref[...] = jnp.zeros_like(acc_ref)

    h = h_ref[...]
    g = _dot(h, wg_ref[...])
    u = _dot(h, wu_ref[...])
    a = (g * jax.nn.sigmoid(g) * u).astype(BF16)
    acc_ref[...] += _dot(a, wd_ref[...])

    @pl.when(f == n_f - 1)
    def _():
        x = xres_ref[...] if has_mix else x_ref[...]
        o_ref[...] = x + 0.5 * _rms(acc_ref[...], post_ref[...])


def _ffn(x, pre_g, wg, wu, wd, post_g, mix=None, *, tm, tf):
    n, d = x.shape
    n_f = D_FF // tf
    row = lambda i, f: (i, 0)
    const = lambda i, f: (0, 0)
    in_specs = [
        pl.BlockSpec((tm, d), row),
        pl.BlockSpec((1, d), const),
        pl.BlockSpec((d, tf), lambda i, f: (0, f)),
        pl.BlockSpec((d, tf), lambda i, f: (0, f)),
        pl.BlockSpec((tf, d), lambda i, f: (f, 0)),
        pl.BlockSpec((1, d), const),
    ]
    args = [x, pre_g, wg, wu, wd, post_g]
    scratch = [pltpu.VMEM((tm, d), BF16), pltpu.VMEM((tm, d), F32)]
    if mix is not None:
        merged, w_out, mix_post_g = mix
        in_specs = [pl.BlockSpec((tm, d), row), pl.BlockSpec((d, d), const),
                    pl.BlockSpec((1, d), const)] + in_specs
        args = [merged, w_out, mix_post_g] + args
        scratch.append(pltpu.VMEM((tm, d), F32))
    return pl.pallas_call(
        functools.partial(_ffn_kernel, n_f=n_f, has_mix=mix is not None),
        grid=(n // tm, n_f),
        in_specs=in_specs,
        out_specs=pl.BlockSpec((tm, d), row),
        out_shape=jax.ShapeDtypeStruct((n, d), F32),
        scratch_shapes=scratch,
        compiler_params=pltpu.CompilerParams(
            dimension_semantics=("parallel", "arbitrary"), vmem_limit_bytes=VMEM_LIMIT_BYTES),
        name="ffn_mix" if mix is not None else "ffn",
    )(*args)


def _cumsum_rows(lf, carry):
    rows = lf.shape[0]
    r = lax.broadcasted_iota(jnp.int32, (CUMSUM_ROWS, CUMSUM_ROWS), 0)
    c = lax.broadcasted_iota(jnp.int32, (CUMSUM_ROWS, CUMSUM_ROWS), 1)
    tri = (c <= r).astype(BF16)
    out = []
    for b in range(rows // CUMSUM_ROWS):
        blk = lf[b * CUMSUM_ROWS:(b + 1) * CUMSUM_ROWS]
        t1 = blk.astype(BF16)
        r1 = blk - t1.astype(F32)
        t2 = r1.astype(BF16)
        t3 = (r1 - t2.astype(F32)).astype(BF16)
        cs = (_dot(tri, t1) + _dot(tri, t2)) + _dot(tri, t3) + carry
        carry = cs[CUMSUM_ROWS - 1:CUMSUM_ROWS, :]
        out.append(cs)
    return jnp.concatenate(out, axis=0), carry


def _proj_kernel(x_ref, pre_ref, wmain_ref, wf_ref, bf_ref, lng_ref, lnb_ref, ws_ref, bs_ref,
                 qt_ref, k_ref, vt_ref, ga_ref, gob_ref,
                 h_ref, cpart_ref, carry_ref, *, tm, blocks_per_seq):
    i = pl.program_id(0)
    c = pl.program_id(1)
    lane = lax.broadcasted_iota(jnp.int32, (tm, LANES), 1)
    in_q_parts = lane < Q_PART_BASE + C_PARTS * N_HEADS
    in_k_parts = (lane >= K_PART_BASE) & (lane < K_PART_BASE + C_PARTS * N_HEADS)

    @pl.when(c == 0)
    def _():
        h = _rms(x_ref[...], pre_ref[...]).astype(BF16)
        h_ref[...] = h

        @pl.when(i % blocks_per_seq == 0)
        def _():
            carry_ref[...] = jnp.zeros_like(carry_ref)

        lf = jax.nn.log_sigmoid(_dot(h, wf_ref[...]) + bf_ref[...])
        lf = jnp.where(in_q_parts | in_k_parts, lf, 0.0)
        cs, carry = _cumsum_rows(lf, carry_ref[0:1, :])
        carry_ref[0:1, :] = carry
        cs = cs * LOG2E
        hi = cs.astype(BF16).astype(F32)
        r1 = cs - hi
        mid = r1.astype(BF16).astype(F32)
        lo = r1 - mid
        part = (lane % 32) // N_HEADS
        cpart_ref[...] = jnp.where(part == 0, hi, jnp.where(part == 1, mid, lo))

    z = _dot(h_ref[...], wmain_ref[...])
    sec = HEADS_PER_STEP * HEAD_DIM

    def section(s, j):
        return z[:, s * sec + j * HEAD_DIM: s * sec + (j + 1) * HEAD_DIM]

    cpart = cpart_ref[...]
    scale = LOG2E / math.sqrt(HEAD_DIM)
    wr = lax.broadcasted_iota(jnp.int32, (SGU_LEN, SGU_LEN), 0)
    wc = lax.broadcasted_iota(jnp.int32, (SGU_LEN, SGU_LEN), 1)
    chunk_causal = (wc // CHUNK) <= (wr // CHUNK)
    for j in range(HEADS_PER_STEP):
        head = c * HEADS_PER_STEP + j
        is_head = (lane % N_HEADS) == head
        q_aug = jnp.where(in_q_parts, cpart, jnp.where(in_k_parts & is_head, 1.0, 0.0))
        k_aug = jnp.where(in_k_parts, -cpart, jnp.where(in_q_parts & is_head, 1.0, 0.0))
        qt_ref[0, j, 0:HEAD_DIM, :] = (section(0, j) * scale).T.astype(BF16)
        qt_ref[0, j, HEAD_DIM:QK_WIDTH, :] = q_aug.T.astype(BF16)
        k_ref[0, j, :, 0:HEAD_DIM] = section(1, j).astype(BF16)
        k_ref[0, j, :, HEAD_DIM:QK_WIDTH] = k_aug.astype(BF16)
        vt = section(2, j).T.astype(BF16)
        for r in range(tm // KV_TILE):
            vt_ref[0, j, r] = vt[:, r * KV_TILE:(r + 1) * KV_TILE]

        cols = slice(j * GROUP_DIM, (j + 1) * GROUP_DIM)
        u = _gelu(section(3, j))
        sv = _gelu(section(4, j))
        mu = jnp.mean(sv, axis=-1, keepdims=True)
        dv = sv - mu
        var = jnp.mean(dv * dv, axis=-1, keepdims=True)
        vn = (dv * lax.rsqrt(var + LN_EPS) * lng_ref[:, cols] + lnb_ref[:, cols]).astype(BF16)
        w = jnp.where(chunk_causal, ws_ref[j], 0.0).astype(BF16)
        b = bs_ref[j]
        mixed = [
            _dot(w, vn[n * SGU_LEN:(n + 1) * SGU_LEN, :]) + b for n in range(tm // SGU_LEN)
        ]
        mixed = jnp.concatenate(mixed, axis=0)
        ga_ref[:, cols] = jax.nn.sigmoid(section(5, j)).astype(BF16)
        gob_ref[:, cols] = (jax.nn.sigmoid(section(6, j)) * u * mixed).astype(BF16)


def _proj(x1, pre_g, wmain, wf, bf, ln_g, ln_b, w_s, b_s, *, batch, seq, tm):
    n, d = x1.shape
    n_c = N_HEADS // HEADS_PER_STEP
    bps = seq // tm
    hp = HEADS_PER_STEP
    return pl.pallas_call(
        functools.partial(_proj_kernel, tm=tm, blocks_per_seq=bps),
        grid=(n // tm, n_c),
        in_specs=[
            pl.BlockSpec((tm, d), lambda i, c: (i, 0)),
            pl.BlockSpec((1, d), lambda i, c: (0, 0)),
            pl.BlockSpec((d, CHUNK_COLS), lambda i, c: (0, c)),
            pl.BlockSpec((d, LANES), lambda i, c: (0, 0)),
            pl.BlockSpec((1, LANES), lambda i, c: (0, 0)),
            pl.BlockSpec((1, hp * GROUP_DIM), lambda i, c: (0, c)),
            pl.BlockSpec((1, hp * GROUP_DIM), lambda i, c: (0, c)),
            pl.BlockSpec((hp, SGU_LEN, SGU_LEN), lambda i, c: (c, 0, 0)),
            pl.BlockSpec((hp, SGU_LEN, 1), lambda i, c: (c, 0, 0)),
        ],
        out_specs=[
            pl.BlockSpec((1, hp, QK_WIDTH, tm), lambda i, c: (i // bps, c, 0, i % bps)),
            pl.BlockSpec((1, hp, tm, QK_WIDTH), lambda i, c: (i // bps, c, i % bps, 0)),
            pl.BlockSpec((1, hp, tm // KV_TILE, HEAD_DIM, KV_TILE),
                         lambda i, c: (i // bps, c, i % bps, 0, 0)),
            pl.BlockSpec((tm, hp * GROUP_DIM), lambda i, c: (i, c)),
            pl.BlockSpec((tm, hp * GROUP_DIM), lambda i, c: (i, c)),
        ],
        out_shape=[
            jax.ShapeDtypeStruct((batch, N_HEADS, QK_WIDTH, seq), BF16),
            jax.ShapeDtypeStruct((batch, N_HEADS, seq, QK_WIDTH), BF16),
            jax.ShapeDtypeStruct((batch, N_HEADS, seq // KV_TILE, HEAD_DIM, KV_TILE), BF16),
            jax.ShapeDtypeStruct((n, d), BF16),
            jax.ShapeDtypeStruct((n, d), BF16),
        ],
        scratch_shapes=[pltpu.VMEM((tm, d), BF16), pltpu.VMEM((tm, LANES), F32),
                        pltpu.VMEM((8, LANES), F32)],
        compiler_params=pltpu.CompilerParams(
            dimension_semantics=("arbitrary", "arbitrary"), vmem_limit_bytes=VMEM_LIMIT_BYTES),
        name="proj",
    )(x1, pre_g, wmain, wf, bf, ln_g, ln_b, w_s, b_s)


def _attn_kernel(qt_ref, k_ref, vt_ref, ga_ref, gob_ref, o_ref, acc_ref, l_ref, kn_ref):
    i = pl.program_id(2)
    tk, tq, half = KV_TILE, Q_TILE, Q_TILE // 2
    n_kv = k_ref.shape[2] // tk

    def kv_rows(j):
        return k_ref[0, 0, pl.ds(pl.multiple_of(j * tk, tk), tk), :]

    def causal(s):
        kv_pos = lax.broadcasted_iota(jnp.int32, s.shape, 0)
        q_pos = lax.broadcasted_iota(jnp.int32, s.shape, 1)
        return jnp.where(kv_pos <= q_pos, s, MASK_VALUE)

    @pl.when(i == 0)
    def _():
        def tile_norm(j, best):
            kk = kv_rows(j)[:, 0:HEAD_DIM].astype(F32)
            return jnp.maximum(best, jnp.sum(kk * kk, axis=1, keepdims=True))
        best = lax.fori_loop(0, n_kv, tile_norm, jnp.zeros((tk, 1), F32))
        kn_ref[...] = jnp.broadcast_to(jnp.max(best, axis=0, keepdims=True), kn_ref.shape)

    qf = qt_ref[0, 0, 0:HEAD_DIM, :].astype(F32)
    bound = jnp.sqrt(jnp.sum(qf * qf, axis=0, keepdims=True) * kn_ref[0:1, 0:1])
    safe = 2.0 * jnp.max(bound) <= SAFE_EXP2_RANGE

    acc_ref[...] = jnp.zeros_like(acc_ref)
    d = 2 * i

    @pl.when(safe)
    def _():
        shift = bound + 1.0

        def tile_pair(t, l):
            kb = k_ref[0, 0, pl.ds(pl.multiple_of(t * tq, tq), tq), :]
            p = jnp.exp2(_dot(kb, qt_ref[0, 0]) - shift)
            vt = jnp.concatenate([vt_ref[0, 0, 2 * t], vt_ref[0, 0, 2 * t + 1]], axis=1)
            acc_ref[...] += _dot(vt, p.astype(BF16))
            return l + jnp.sum(p, axis=0, keepdims=True)

        l = lax.fori_loop(0, i, tile_pair, jnp.zeros((1, tq), F32))
        p = jnp.exp2(causal(_dot(kv_rows(d), qt_ref[0, 0])) - shift)
        acc_ref[...] += _dot(vt_ref[0, 0, d], p.astype(BF16))
        l = l + jnp.sum(p, axis=0, keepdims=True)
        p = jnp.exp2(causal(_dot(kv_rows(d + 1), qt_ref[0, 0, :, half:])) - shift[:, half:])
        acc_ref[:, half:] += _dot(vt_ref[0, 0, d + 1], p.astype(BF16))
        l_ref[0:1, :] = jnp.concatenate(
            [l[:, :half], l[:, half:] + jnp.sum(p, axis=0, keepdims=True)], axis=1)

    @pl.when(jnp.logical_not(safe))
    def _():
        def online(s, j, carry, cols):
            m, l = carry
            m_new = jnp.maximum(m, jnp.max(s, axis=0, keepdims=True))
            alpha = jnp.exp2(m - m_new)
            p = jnp.exp2(s - m_new)
            acc_ref[:, cols] = alpha * acc_ref[:, cols] + _dot(vt_ref[0, 0, j], p.astype(BF16))
            return m_new, alpha * l + jnp.sum(p, axis=0, keepdims=True)

        every = slice(0, tq)
        init = (jnp.full((1, tq), MASK_VALUE, F32), jnp.zeros((1, tq), F32))
        carry = lax.fori_loop(
            0, d, lambda j, c: online(_dot(kv_rows(j), qt_ref[0, 0]), j, c, every), init)
        m, l = online(causal(_dot(kv_rows(d), qt_ref[0, 0])), d, carry, every)
        upper = slice(half, tq)
        _, l_hi = online(causal(_dot(kv_rows(d + 1), qt_ref[0, 0, :, half:])), d + 1,
                         (m[:, half:], l[:, half:]), upper)
        l_ref[0:1, :] = jnp.concatenate([l[:, :half], l_hi], axis=1)

    o = (acc_ref[...] / l_ref[0:1, :]).T
    o_ref[...] = (ga_ref[...].astype(F32) * o + gob_ref[...].astype(F32)).astype(BF16)


def _attn(qt, k, vt, ga, gob):
    batch, n_heads, seq, _ = k.shape
    tq = Q_TILE
    nq = seq // tq
    tok_blk = lambda b, h, i: (b * nq + i, h)
    return pl.pallas_call(
        _attn_kernel,
        grid=(batch, n_heads, nq),
        scratch_shapes=[pltpu.VMEM((HEAD_DIM, tq), F32), pltpu.VMEM((8, tq), F32),
                        pltpu.VMEM((8, LANES), F32)],
        in_specs=[
            pl.BlockSpec((1, 1, QK_WIDTH, tq), lambda b, h, i: (b, h, 0, i)),
            pl.BlockSpec((1, 1, seq, QK_WIDTH), lambda b, h, i: (b, h, 0, 0)),
            pl.BlockSpec((1, 1, seq // KV_TILE, HEAD_DIM, KV_TILE), lambda b, h, i: (b, h, 0, 0, 0)),
            pl.BlockSpec((tq, HEAD_DIM), tok_blk),
            pl.BlockSpec((tq, HEAD_DIM), tok_blk),
        ],
        out_specs=pl.BlockSpec((tq, HEAD_DIM), tok_blk),
        out_shape=jax.ShapeDtypeStruct(ga.shape, BF16),
        compiler_params=pltpu.CompilerParams(
            dimension_semantics=("parallel", "parallel", "arbitrary"),
            vmem_limit_bytes=VMEM_LIMIT_BYTES),
        name="attn",
    )(qt, k, vt, ga, gob)


def _pack_w_in(w_in):
    d = w_in.shape[0]
    col_f = 3 * D_MODEL
    starts = [0, D_MODEL, 2 * D_MODEL, col_f + N_HEADS, col_f + N_HEADS + D_MODEL,
              col_f + N_HEADS + 2 * D_MODEL, col_f + N_HEADS + 3 * D_MODEL]
    n_c = N_HEADS // HEADS_PER_STEP
    secs = [w_in[:, s:s + D_MODEL].reshape(d, n_c, 1, HEADS_PER_STEP * HEAD_DIM) for s in starts]
    wmain = jnp.concatenate(secs, axis=2).reshape(d, n_c * CHUNK_COLS).astype(BF16)
    wf = w_in[:, col_f:col_f + N_HEADS]
    return wmain, _replicate_forget(wf).astype(BF16)


def _replicate_forget(a):
    rows = a.shape[0]
    rep = jnp.tile(a, (1, C_PARTS))
    gap = jnp.zeros((rows, K_PART_BASE - C_PARTS * N_HEADS), a.dtype)
    tail = jnp.zeros((rows, LANES - K_PART_BASE - C_PARTS * N_HEADS), a.dtype)
    return jnp.concatenate([rep, gap, rep, tail], axis=1)


def kernel(x, ffn1_pre_g, ffn1_w_gate, ffn1_w_up, ffn1_w_down, ffn1_post_g, mix_pre_g, w_in, b_forget, sgu_ln_g, sgu_ln_b, sgu_w_s, sgu_b_s, w_out, mix_post_g, ffn2_pre_g, ffn2_w_gate, ffn2_w_up, ffn2_w_down, ffn2_post_g):
    batch, seq, d = x.shape
    depth = ffn1_pre_g.shape[0]
    tm_ffn = min(1024, seq)
    tm_proj = min(1024, seq)
    xf = x.reshape(batch * seq, d)
    for l in range(depth):
        row = lambda a: a[l].reshape(1, -1)
        xf = _ffn(xf, row(ffn1_pre_g), ffn1_w_gate[l].astype(BF16), ffn1_w_up[l].astype(BF16),
                  ffn1_w_down[l].astype(BF16), row(ffn1_post_g), tm=tm_ffn, tf=512)
        wmain, wf = _pack_w_in(w_in[l])
        qt, k, vt, ga, gob = _proj(
            xf, row(mix_pre_g), wmain, wf, _replicate_forget(row(b_forget)),
            row(sgu_ln_g), row(sgu_ln_b), sgu_w_s[l], sgu_b_s[l].reshape(N_HEADS, SGU_LEN, 1),
            batch=batch, seq=seq, tm=tm_proj)
        merged = _attn(qt, k, vt, ga, gob)
        xf = _ffn(xf, row(ffn2_pre_g), ffn2_w_gate[l].astype(BF16), ffn2_w_up[l].astype(BF16),
                  ffn2_w_down[l].astype(BF16), row(ffn2_post_g),
                  mix=(merged, w_out[l].astype(BF16), row(mix_post_g)), tm=tm_ffn, tf=512)
    return xf.reshape(batch, seq, d)
```

```python
import functools
import math

import jax
import jax.numpy as jnp
from jax import lax
from jax.experimental import pallas as pl
from jax.experimental.pallas import tpu as pltpu

D_MODEL = 1024
D_FF = 4 * D_MODEL
HEAD_DIM = 128
N_HEADS = D_MODEL // HEAD_DIM
GROUP_DIM = 128
SGU_LEN = 128
CHUNK = 64
RMS_EPS = 1e-6
LN_EPS = 1e-5

LANES = 128
HEADS_PER_STEP = 2
N_SECTIONS = 7
CHUNK_COLS = N_SECTIONS * HEADS_PER_STEP * HEAD_DIM
QK_WIDTH = 2 * HEAD_DIM
C_PARTS = 3
Q_PART_BASE = 0
K_PART_BASE = 32
FFN_ROWS = 512
FFN_CHUNK = 1024
PROJ_ROWS = 1024
CUMSUM_ROWS = 256
KV_TILE = 512
Q_TILE = 2 * KV_TILE
Q_NORM_LANE = 64
K_NORM_LANE = 65
KQ_NORM_LANE = 66
NORM_INFLATE = 1.02
MASK_VALUE = -1e30
LOG2E = math.log2(math.e)
SAFE_EXP2_RANGE = 64.0
VMEM_LIMIT_BYTES = 56 * 1024 * 1024

F32 = jnp.float32
BF16 = jnp.bfloat16


def _rms(x, g):
    return x * lax.rsqrt(jnp.mean(x * x, axis=-1, keepdims=True) + RMS_EPS) * g


def _dot(a, b):
    return jnp.dot(a, b, preferred_element_type=F32)


def _gelu(x):
    return 0.5 * x * (1.0 + lax.erf(x * math.sqrt(0.5)))


def _ffn_kernel(*refs, tf, has_mix, has_next):
    refs = list(refs)
    if has_mix:
        m_ref, wo_ref, mpost_ref = refs[:3]
        refs = refs[3:]
    x_ref, pre_ref, wg_ref, wu_ref, wd_ref, post_ref = refs[:6]
    refs = refs[6:]
    if has_next:
        next_g_ref, o_ref, hn_ref = refs
    else:
        (o_ref,) = refs
    x = x_ref[...]
    if has_mix:
        x = x + _rms(_dot(m_ref[...], wo_ref[...]), mpost_ref[...])
    h = _rms(x, pre_ref[...]).astype(BF16)
    y = None
    for f in range(D_FF // tf):
        cols = slice(f * tf, (f + 1) * tf)
        g = _dot(h, wg_ref[:, cols])
        u = _dot(h, wu_ref[:, cols])
        a = (g * jax.nn.sigmoid(g) * u).astype(BF16)
        part = _dot(a, wd_ref[cols, :])
        y = part if y is None else y + part
    out = x + 0.5 * _rms(y, post_ref[...])
    o_ref[...] = out
    if has_next:
        hn_ref[...] = _rms(out, next_g_ref[...]).astype(BF16)


def _ffn(x, pre_g, wg, wu, wd, post_g, mix=None, next_g=None, *, tm, tf):
    n, d = x.shape
    row = lambda i: (i, 0)
    resident = lambda shape: pl.BlockSpec(shape, lambda i: (0, 0), pipeline_mode=pl.Buffered(1))
    in_specs = [
        pl.BlockSpec((tm, d), row),
        resident((1, d)),
        resident((d, D_FF)),
        resident((d, D_FF)),
        resident((D_FF, d)),
        resident((1, d)),
    ]
    args = [x, pre_g, wg, wu, wd, post_g]
    if mix is not None:
        merged, w_out, mix_post_g = mix
        in_specs = [pl.BlockSpec((tm, d), row), resident((d, d)), resident((1, d))] + in_specs
        args = [merged, w_out, mix_post_g] + args
    out_specs = pl.BlockSpec((tm, d), row)
    out_shape = jax.ShapeDtypeStruct((n, d), F32)
    if next_g is not None:
        in_specs.append(resident((1, d)))
        args.append(next_g)
        out_specs = [out_specs, pl.BlockSpec((tm, d), row)]
        out_shape = [out_shape, jax.ShapeDtypeStruct((n, d), BF16)]
    return pl.pallas_call(
        functools.partial(_ffn_kernel, tf=tf, has_mix=mix is not None,
                          has_next=next_g is not None),
        grid=(n // tm,),
        in_specs=in_specs,
        out_specs=out_specs,
        out_shape=out_shape,
        compiler_params=pltpu.CompilerParams(
            dimension_semantics=("parallel",), vmem_limit_bytes=VMEM_LIMIT_BYTES),
        name="ffn_mix" if mix is not None else "ffn",
    )(*args)


def _cumsum_rows(lf, carry):
    rows = lf.shape[0]
    r = lax.broadcasted_iota(jnp.int32, (CUMSUM_ROWS, CUMSUM_ROWS), 0)
    c = lax.broadcasted_iota(jnp.int32, (CUMSUM_ROWS, CUMSUM_ROWS), 1)
    tri = (c <= r).astype(BF16)
    out = []
    for b in range(rows // CUMSUM_ROWS):
        blk = lf[b * CUMSUM_ROWS:(b + 1) * CUMSUM_ROWS]
        t1 = blk.astype(BF16)
        r1 = blk - t1.astype(F32)
        t2 = r1.astype(BF16)
        t3 = (r1 - t2.astype(F32)).astype(BF16)
        cs = (_dot(tri, t1) + _dot(tri, t2)) + _dot(tri, t3) + carry
        carry = cs[CUMSUM_ROWS - 1:CUMSUM_ROWS, :]
        out.append(cs)
    return jnp.concatenate(out, axis=0), carry


def _proj_kernel(h_ref, wmain_ref, wf_ref, bf_ref, lng_ref, lnb_ref, ws_ref, bs_ref,
                 qt_ref, k_ref, vt_ref, ga_ref, gob_ref,
                 cpart_ref, carry_ref, *, tm, blocks_per_seq):
    i = pl.program_id(0)
    c = pl.program_id(1)
    lane = lax.broadcasted_iota(jnp.int32, (tm, LANES), 1)
    in_q_parts = lane < Q_PART_BASE + C_PARTS * N_HEADS
    in_k_parts = (lane >= K_PART_BASE) & (lane < K_PART_BASE + C_PARTS * N_HEADS)

    @pl.when(c == 0)
    def _():
        @pl.when(i % blocks_per_seq == 0)
        def _():
            carry_ref[...] = jnp.zeros_like(carry_ref)

        lf = jax.nn.log_sigmoid(_dot(h_ref[...], wf_ref[...]) + bf_ref[...])
        lf = jnp.where(in_q_parts | in_k_parts, lf, 0.0)
        cs, carry = _cumsum_rows(lf, carry_ref[0:1, :])
        carry_ref[0:1, :] = carry
        cs = cs * LOG2E
        hi = cs.astype(BF16).astype(F32)
        r1 = cs - hi
        mid = r1.astype(BF16).astype(F32)
        lo = r1 - mid
        part = (lane % 32) // N_HEADS
        cpart_ref[...] = jnp.where(part == 0, hi, jnp.where(part == 1, mid, lo))

    z = _dot(h_ref[...], wmain_ref[...])
    sec = HEADS_PER_STEP * HEAD_DIM

    def section(s, j):
        return z[:, s * sec + j * HEAD_DIM: s * sec + (j + 1) * HEAD_DIM]

    cpart = cpart_ref[...]
    scale = LOG2E / math.sqrt(HEAD_DIM)
    wr = lax.broadcasted_iota(jnp.int32, (SGU_LEN, SGU_LEN), 0)
    wc = lax.broadcasted_iota(jnp.int32, (SGU_LEN, SGU_LEN), 1)
    chunk_causal = (wc // CHUNK) <= (wr // CHUNK)
    for j in range(HEADS_PER_STEP):
        head = c * HEADS_PER_STEP + j
        is_head = (lane % N_HEADS) == head
        q = section(0, j) * scale
        k = section(1, j)
        qn = jnp.sum(q * q, axis=-1, keepdims=True) * NORM_INFLATE
        kn = jnp.sum(k * k, axis=-1, keepdims=True) * NORM_INFLATE
        q_aug = jnp.where(in_q_parts, cpart, jnp.where(in_k_parts & is_head, 1.0, 0.0))
        q_aug = jnp.where(lane == Q_NORM_LANE, qn, q_aug)
        k_aug = jnp.where(in_k_parts, -cpart, jnp.where(in_q_parts & is_head, 1.0, 0.0))
        k_aug = jnp.where(lane == K_NORM_LANE, kn, jnp.where(lane == KQ_NORM_LANE, qn, k_aug))
        qt_ref[0, j, 0:HEAD_DIM, :] = q.T.astype(BF16)
        qt_ref[0, j, HEAD_DIM:QK_WIDTH, :] = q_aug.T.astype(BF16)
        k_ref[0, j, :, 0:HEAD_DIM] = k.astype(BF16)
        k_ref[0, j, :, HEAD_DIM:QK_WIDTH] = k_aug.astype(BF16)
        vt = section(2, j).T.astype(BF16)
        for r in range(tm // KV_TILE):
            vt_ref[0, j, r] = vt[:, r * KV_TILE:(r + 1) * KV_TILE]

        cols = slice(j * GROUP_DIM, (j + 1) * GROUP_DIM)
        u = _gelu(section(3, j))
        sv = _gelu(section(4, j))
        mu = jnp.mean(sv, axis=-1, keepdims=True)
        dv = sv - mu
        var = jnp.mean(dv * dv, axis=-1, keepdims=True)
        vn = (dv * lax.rsqrt(var + LN_EPS) * lng_ref[:, cols] + lnb_ref[:, cols]).astype(BF16)
        w = jnp.where(chunk_causal, ws_ref[j], 0.0).astype(BF16)
        b = bs_ref[j]
        mixed = [
            _dot(w, vn[n * SGU_LEN:(n + 1) * SGU_LEN, :]) + b for n in range(tm // SGU_LEN)
        ]
        mixed = jnp.concatenate(mixed, axis=0)
        ga_ref[:, cols] = jax.nn.sigmoid(section(5, j)).astype(BF16)
        gob_ref[:, cols] = (jax.nn.sigmoid(section(6, j)) * u * mixed).astype(BF16)


def _proj(h, wmain, wf, bf, ln_g, ln_b, w_s, b_s, *, batch, seq, tm):
    n, d = h.shape
    n_c = N_HEADS // HEADS_PER_STEP
    bps = seq // tm
    hp = HEADS_PER_STEP
    return pl.pallas_call(
        functools.partial(_proj_kernel, tm=tm, blocks_per_seq=bps),
        grid=(n // tm, n_c),
        in_specs=[
            pl.BlockSpec((tm, d), lambda i, c: (i, 0)),
            pl.BlockSpec((d, CHUNK_COLS), lambda i, c: (0, c)),
            pl.BlockSpec((d, LANES), lambda i, c: (0, 0)),
            pl.BlockSpec((1, LANES), lambda i, c: (0, 0)),
            pl.BlockSpec((1, hp * GROUP_DIM), lambda i, c: (0, c)),
            pl.BlockSpec((1, hp * GROUP_DIM), lambda i, c: (0, c)),
            pl.BlockSpec((hp, SGU_LEN, SGU_LEN), lambda i, c: (c, 0, 0)),
            pl.BlockSpec((hp, SGU_LEN, 1), lambda i, c: (c, 0, 0)),
        ],
        out_specs=[
            pl.BlockSpec((1, hp, QK_WIDTH, tm), lambda i, c: (i // bps, c, 0, i % bps)),
            pl.BlockSpec((1, hp, tm, QK_WIDTH), lambda i, c: (i // bps, c, i % bps, 0)),
            pl.BlockSpec((1, hp, tm // KV_TILE, HEAD_DIM, KV_TILE),
                         lambda i, c: (i // bps, c, i % bps, 0, 0)),
            pl.BlockSpec((tm, hp * GROUP_DIM), lambda i, c: (i, c)),
            pl.BlockSpec((tm, hp * GROUP_DIM), lambda i, c: (i, c)),
        ],
        out_shape=[
            jax.ShapeDtypeStruct((batch, N_HEADS, QK_WIDTH, seq), BF16),
            jax.ShapeDtypeStruct((batch, N_HEADS, seq, QK_WIDTH), BF16),
            jax.ShapeDtypeStruct((batch, N_HEADS, seq // KV_TILE, HEAD_DIM, KV_TILE), BF16),
            jax.ShapeDtypeStruct((n, d), BF16),
            jax.ShapeDtypeStruct((n, d), BF16),
        ],
        scratch_shapes=[pltpu.VMEM((tm, LANES), F32), pltpu.VMEM((8, LANES), F32)],
        compiler_params=pltpu.CompilerParams(
            dimension_semantics=("arbitrary", "arbitrary"), vmem_limit_bytes=VMEM_LIMIT_BYTES),
        name="proj",
    )(h, wmain, wf, bf, ln_g, ln_b, w_s, b_s)


def _attn_kernel(qt_ref, k_ref, vt_ref, ga_ref, gob_ref, o_ref, acc_ref, l_ref, kn_ref, safe_ref):
    i = pl.program_id(2)
    tk, tq, half = KV_TILE, Q_TILE, Q_TILE // 2
    n_kv = k_ref.shape[2] // tk

    def kv_rows(j):
        return k_ref[0, 0, pl.ds(pl.multiple_of(j * tk, tk), tk), :]

    def causal(s):
        kv_pos = lax.broadcasted_iota(jnp.int32, s.shape, 0)
        q_pos = lax.broadcasted_iota(jnp.int32, s.shape, 1)
        return jnp.where(kv_pos <= q_pos, s, MASK_VALUE)

    @pl.when(i == 0)
    def _():
        def tile_max(j, best):
            return jnp.maximum(best, kv_rows(j)[:, HEAD_DIM:QK_WIDTH].astype(F32))
        best = lax.fori_loop(0, n_kv, tile_max, jnp.zeros((tk, LANES), F32))
        best = jnp.max(best, axis=0, keepdims=True)
        lane = lax.broadcasted_iota(jnp.int32, (1, LANES), 1)
        kn = jnp.max(jnp.where(lane == K_NORM_LANE, best, 0.0), axis=1, keepdims=True)
        qn = jnp.max(jnp.where(lane == KQ_NORM_LANE, best, 0.0), axis=1, keepdims=True)
        kn_ref[...] = jnp.broadcast_to(kn, kn_ref.shape)
        safe_ref[0] = (2.0 * jnp.sqrt(jnp.max(kn * qn)) <= SAFE_EXP2_RANGE).astype(jnp.int32)

    qn_row = qt_ref[0, 0, HEAD_DIM + Q_NORM_LANE:HEAD_DIM + Q_NORM_LANE + 1, :].astype(F32)
    bound = jnp.sqrt(qn_row * kn_ref[0:1, 0:1])
    safe = safe_ref[0] == 1

    acc_ref[...] = jnp.zeros_like(acc_ref)
    d = 2 * i

    @pl.when(safe)
    def _():
        shift = bound + 1.0

        def tile_pair(t, l):
            kb = k_ref[0, 0, pl.ds(pl.multiple_of(t * tq, tq), tq), :]
            p = jnp.exp2(_dot(kb, qt_ref[0, 0]) - shift)
            vt = jnp.concatenate([vt_ref[0, 0, 2 * t], vt_ref[0, 0, 2 * t + 1]], axis=1)
            acc_ref[...] += _dot(vt, p.astype(BF16))
            return l + jnp.sum(p, axis=0, keepdims=True)

        l = lax.fori_loop(0, i, tile_pair, jnp.zeros((1, tq), F32))
        p = jnp.exp2(causal(_dot(kv_rows(d), qt_ref[0, 0])) - shift)
        acc_ref[...] += _dot(vt_ref[0, 0, d], p.astype(BF16))
        l = l + jnp.sum(p, axis=0, keepdims=True)
        p = jnp.exp2(causal(_dot(kv_rows(d + 1), qt_ref[0, 0, :, half:])) - shift[:, half:])
        acc_ref[:, half:] += _dot(vt_ref[0, 0, d + 1], p.astype(BF16))
        l_ref[0:1, :] = jnp.concatenate(
            [l[:, :half], l[:, half:] + jnp.sum(p, axis=0, keepdims=True)], axis=1)

    @pl.when(jnp.logical_not(safe))
    def _():
        def online(s, j, carry, cols):
            m, l = carry
            m_new = jnp.maximum(m, jnp.max(s, axis=0, keepdims=True))
            alpha = jnp.exp2(m - m_new)
            p = jnp.exp2(s - m_new)
            acc_ref[:, cols] = alpha * acc_ref[:, cols] + _dot(vt_ref[0, 0, j], p.astype(BF16))
            return m_new, alpha * l + jnp.sum(p, axis=0, keepdims=True)

        every = slice(0, tq)
        init = (jnp.full((1, tq), MASK_VALUE, F32), jnp.zeros((1, tq), F32))
        carry = lax.fori_loop(
            0, d, lambda j, c: online(_dot(kv_rows(j), qt_ref[0, 0]), j, c, every), init)
        m, l = online(causal(_dot(kv_rows(d), qt_ref[0, 0])), d, carry, every)
        upper = slice(half, tq)
        _, l_hi = online(causal(_dot(kv_rows(d + 1), qt_ref[0, 0, :, half:])), d + 1,
                         (m[:, half:], l[:, half:]), upper)
        l_ref[0:1, :] = jnp.concatenate([l[:, :half], l_hi], axis=1)

    o = (acc_ref[...] / l_ref[0:1, :]).T
    o_ref[...] = (ga_ref[...].astype(F32) * o + gob_ref[...].astype(F32)).astype(BF16)


def _attn(qt, k, vt, ga, gob):
    batch, n_heads, seq, _ = k.shape
    tq = Q_TILE
    nq = seq // tq
    tok_blk = lambda b, h, i: (b * nq + i, h)
    return pl.pallas_call(
        _attn_kernel,
        grid=(batch, n_heads, nq),
        scratch_shapes=[pltpu.VMEM((HEAD_DIM, tq), F32), pltpu.VMEM((8, tq), F32),
                        pltpu.VMEM((8, LANES), F32), pltpu.SMEM((1,), jnp.int32)],
        in_specs=[
            pl.BlockSpec((1, 1, QK_WIDTH, tq), lambda b, h, i: (b, h, 0, i)),
            pl.BlockSpec((1, 1, seq, QK_WIDTH), lambda b, h, i: (b, h, 0, 0)),
            pl.BlockSpec((1, 1, seq // KV_TILE, HEAD_DIM, KV_TILE), lambda b, h, i: (b, h, 0, 0, 0)),
            pl.BlockSpec((tq, HEAD_DIM), tok_blk),
            pl.BlockSpec((tq, HEAD_DIM), tok_blk),
        ],
        out_specs=pl.BlockSpec((tq, HEAD_DIM), tok_blk),
        out_shape=jax.ShapeDtypeStruct(ga.shape, BF16),
        compiler_params=pltpu.CompilerParams(
            dimension_semantics=("parallel", "parallel", "arbitrary"),
            vmem_limit_bytes=VMEM_LIMIT_BYTES),
        name="attn",
    )(qt, k, vt, ga, gob)


def _pack_w_in(w_in):
    d = w_in.shape[0]
    col_f = 3 * D_MODEL
    starts = [0, D_MODEL, 2 * D_MODEL, col_f + N_HEADS, col_f + N_HEADS + D_MODEL,
              col_f + N_HEADS + 2 * D_MODEL, col_f + N_HEADS + 3 * D_MODEL]
    n_c = N_HEADS // HEADS_PER_STEP
    secs = [w_in[:, s:s + D_MODEL].reshape(d, n_c, 1, HEADS_PER_STEP * HEAD_DIM) for s in starts]
    wmain = jnp.concatenate(secs, axis=2).reshape(d, n_c * CHUNK_COLS).astype(BF16)
    wf = w_in[:, col_f:col_f + N_HEADS]
    return wmain, _replicate_forget(wf).astype(BF16)


def _replicate_forget(a):
    rows = a.shape[0]
    rep = jnp.tile(a, (1, C_PARTS))
    gap = jnp.zeros((rows, K_PART_BASE - C_PARTS * N_HEADS), a.dtype)
    tail = jnp.zeros((rows, LANES - K_PART_BASE - C_PARTS * N_HEADS), a.dtype)
    return jnp.concatenate([rep, gap, rep, tail], axis=1)


def kernel(x, ffn1_pre_g, ffn1_w_gate, ffn1_w_up, ffn1_w_down, ffn1_post_g, mix_pre_g, w_in, b_forget, sgu_ln_g, sgu_ln_b, sgu_w_s, sgu_b_s, w_out, mix_post_g, ffn2_pre_g, ffn2_w_gate, ffn2_w_up, ffn2_w_down, ffn2_post_g):
    batch, seq, d = x.shape
    depth = ffn1_pre_g.shape[0]
    tm_ffn = min(FFN_ROWS, seq)
    tm_proj = min(PROJ_ROWS, seq)
    xf = x.reshape(batch * seq, d)
    for l in range(depth):
        row = lambda a: a[l].reshape(1, -1)
        xf, h = _ffn(xf, row(ffn1_pre_g), ffn1_w_gate[l].astype(BF16), ffn1_w_up[l].astype(BF16),
                     ffn1_w_down[l].astype(BF16), row(ffn1_post_g), next_g=row(mix_pre_g),
                     tm=tm_ffn, tf=FFN_CHUNK)
        wmain, wf = _pack_w_in(w_in[l])
        qt, k, vt, ga, gob = _proj(
            h, wmain, wf, _replicate_forget(row(b_forget)),
            row(sgu_ln_g), row(sgu_ln_b), sgu_w_s[l], sgu_b_s[l].reshape(N_HEADS, SGU_LEN, 1),
            batch=batch, seq=seq, tm=tm_proj)
        merged = _attn(qt, k, vt, ga, gob)
        xf = _ffn(xf, row(ffn2_pre_g), ffn2_w_gate[l].astype(BF16), ffn2_w_up[l].astype(BF16),
                  ffn2_w_down[l].astype(BF16), row(ffn2_post_g),
                  mix=(merged, w_out[l].astype(BF16), row(mix_post_g)), tm=tm_ffn, tf=FFN_CHUNK)
    return xf.reshape(batch, seq, d)
```

```python
import functools
import math

import jax
import jax.numpy as jnp
from jax import lax
from jax.experimental import pallas as pl
from jax.experimental.pallas import tpu as pltpu

D_MODEL = 1024
D_FF = 4 * D_MODEL
HEAD_DIM = 128
N_HEADS = D_MODEL // HEAD_DIM
GROUP_DIM = 128
SGU_LEN = 128
CHUNK = 64
RMS_EPS = 1e-6
LN_EPS = 1e-5

LANES = 128
HEADS_PER_STEP = 2
N_SECTIONS = 7
CHUNK_COLS = N_SECTIONS * HEADS_PER_STEP * HEAD_DIM
QK_WIDTH = 2 * HEAD_DIM
C_PARTS = 3
Q_PART_BASE = 0
K_PART_BASE = 32
FFN_ROWS = 512
FFN_CHUNK = 1024
FFN_ROW_GROUPS = 2
PROJ_ROWS = 1024
CUMSUM_ROWS = 256
KV_TILE = 512
Q_TILE = 2 * KV_TILE
Q_NORM_LANE = 64
K_NORM_LANE = 65
KQ_NORM_LANE = 66
NORM_INFLATE = 1.02
MASK_VALUE = -1e30
LOG2E = math.log2(math.e)
SAFE_EXP2_RANGE = 64.0
VMEM_LIMIT_BYTES = 56 * 1024 * 1024

F32 = jnp.float32
BF16 = jnp.bfloat16


def _rms(x, g):
    return x * lax.rsqrt(jnp.mean(x * x, axis=-1, keepdims=True) + RMS_EPS) * g


def _dot(a, b):
    return jnp.dot(a, b, preferred_element_type=F32)


def _gelu(x):
    return 0.5 * x * (1.0 + lax.erf(x * math.sqrt(0.5)))


def _ffn_kernel(*refs, tf, has_mix, has_next):
    refs = list(refs)
    if has_mix:
        m_ref, wo_ref, mpost_ref = refs[:3]
        refs = refs[3:]
    x_ref, pre_ref, wg_ref, wu_ref, wd_ref, post_ref = refs[:6]
    refs = refs[6:]
    if has_next:
        next_g_ref, o_ref, hn_ref = refs
    else:
        (o_ref,) = refs
    tm = x_ref.shape[0]
    groups = 1 if has_mix else FFN_ROW_GROUPS
    for r in range(groups):
        rows = slice(r * tm // groups, (r + 1) * tm // groups)
        x = x_ref[rows, :]
        if has_mix:
            x = x + _rms(_dot(m_ref[rows, :], wo_ref[...]), mpost_ref[...])
        h = _rms(x, pre_ref[...]).astype(BF16)
        y = None
        for f in range(D_FF // tf):
            cols = slice(f * tf, (f + 1) * tf)
            g = _dot(h, wg_ref[:, cols])
            u = _dot(h, wu_ref[:, cols])
            a = (g * jax.nn.sigmoid(g) * u).astype(BF16)
            part = _dot(a, wd_ref[cols, :])
            y = part if y is None else y + part
        out = x + 0.5 * _rms(y, post_ref[...])
        o_ref[rows, :] = out
        if has_next:
            hn_ref[rows, :] = _rms(out, next_g_ref[...]).astype(BF16)


def _ffn(x, pre_g, wg, wu, wd, post_g, mix=None, next_g=None, *, tm, tf):
    n, d = x.shape
    row = lambda i: (i, 0)
    resident = lambda shape: pl.BlockSpec(shape, lambda i: (0, 0), pipeline_mode=pl.Buffered(1))
    in_specs = [
        pl.BlockSpec((tm, d), row),
        resident((1, d)),
        resident((d, D_FF)),
        resident((d, D_FF)),
        resident((D_FF, d)),
        resident((1, d)),
    ]
    args = [x, pre_g, wg, wu, wd, post_g]
    if mix is not None:
        merged, w_out, mix_post_g = mix
        in_specs = [pl.BlockSpec((tm, d), row), resident((d, d)), resident((1, d))] + in_specs
        args = [merged, w_out, mix_post_g] + args
    out_specs = pl.BlockSpec((tm, d), row)
    out_shape = jax.ShapeDtypeStruct((n, d), F32)
    if next_g is not None:
        in_specs.append(resident((1, d)))
        args.append(next_g)
        out_specs = [out_specs, pl.BlockSpec((tm, d), row)]
        out_shape = [out_shape, jax.ShapeDtypeStruct((n, d), BF16)]
    return pl.pallas_call(
        functools.partial(_ffn_kernel, tf=tf, has_mix=mix is not None,
                          has_next=next_g is not None),
        grid=(n // tm,),
        in_specs=in_specs,
        out_specs=out_specs,
        out_shape=out_shape,
        compiler_params=pltpu.CompilerParams(
            dimension_semantics=("parallel",), vmem_limit_bytes=VMEM_LIMIT_BYTES),
        name="ffn_mix" if mix is not None else "ffn",
    )(*args)


def _cumsum_rows(lf, carry):
    rows = lf.shape[0]
    r = lax.broadcasted_iota(jnp.int32, (CUMSUM_ROWS, CUMSUM_ROWS), 0)
    c = lax.broadcasted_iota(jnp.int32, (CUMSUM_ROWS, CUMSUM_ROWS), 1)
    tri = (c <= r).astype(BF16)
    out = []
    for b in range(rows // CUMSUM_ROWS):
        blk = lf[b * CUMSUM_ROWS:(b + 1) * CUMSUM_ROWS]
        t1 = blk.astype(BF16)
        r1 = blk - t1.astype(F32)
        t2 = r1.astype(BF16)
        t3 = (r1 - t2.astype(F32)).astype(BF16)
        cs = (_dot(tri, t1) + _dot(tri, t2)) + _dot(tri, t3) + carry
        carry = cs[CUMSUM_ROWS - 1:CUMSUM_ROWS, :]
        out.append(cs)
    return jnp.concatenate(out, axis=0), carry


def _proj_kernel(h_ref, wmain_ref, wf_ref, bf_ref, lng_ref, lnb_ref, ws_ref, bs_ref,
                 qt_ref, k_ref, vt_ref, ga_ref, gob_ref,
                 cpart_ref, carry_ref, *, tm, blocks_per_seq):
    i = pl.program_id(0)
    c = pl.program_id(1)
    lane = lax.broadcasted_iota(jnp.int32, (tm, LANES), 1)
    in_q_parts = lane < Q_PART_BASE + C_PARTS * N_HEADS
    in_k_parts = (lane >= K_PART_BASE) & (lane < K_PART_BASE + C_PARTS * N_HEADS)

    @pl.when(c == 0)
    def _():
        @pl.when(i % blocks_per_seq == 0)
        def _():
            carry_ref[...] = jnp.zeros_like(carry_ref)

        lf = jax.nn.log_sigmoid(_dot(h_ref[...], wf_ref[...]) + bf_ref[...])
        lf = jnp.where(in_q_parts | in_k_parts, lf, 0.0)
        cs, carry = _cumsum_rows(lf, carry_ref[0:1, :])
        carry_ref[0:1, :] = carry
        cs = cs * LOG2E
        hi = cs.astype(BF16).astype(F32)
        r1 = cs - hi
        mid = r1.astype(BF16).astype(F32)
        lo = r1 - mid
        part = (lane % 32) // N_HEADS
        cpart_ref[...] = jnp.where(part == 0, hi, jnp.where(part == 1, mid, lo))

    z = _dot(h_ref[...], wmain_ref[...])
    sec = HEADS_PER_STEP * HEAD_DIM

    def section(s, j):
        return z[:, s * sec + j * HEAD_DIM: s * sec + (j + 1) * HEAD_DIM]

    cpart = cpart_ref[...]
    scale = LOG2E / math.sqrt(HEAD_DIM)
    wr = lax.broadcasted_iota(jnp.int32, (SGU_LEN, SGU_LEN), 0)
    wc = lax.broadcasted_iota(jnp.int32, (SGU_LEN, SGU_LEN), 1)
    chunk_causal = (wc // CHUNK) <= (wr // CHUNK)
    for j in range(HEADS_PER_STEP):
        head = c * HEADS_PER_STEP + j
        is_head = (lane % N_HEADS) == head
        q = section(0, j) * scale
        k = section(1, j)
        qn = jnp.sum(q * q, axis=-1, keepdims=True) * NORM_INFLATE
        kn = jnp.sum(k * k, axis=-1, keepdims=True) * NORM_INFLATE
        q_aug = jnp.where(in_q_parts, cpart, jnp.where(in_k_parts & is_head, 1.0, 0.0))
        q_aug = jnp.where(lane == Q_NORM_LANE, qn, q_aug)
        k_aug = jnp.where(in_k_parts, -cpart, jnp.where(in_q_parts & is_head, 1.0, 0.0))
        k_aug = jnp.where(lane == K_NORM_LANE, kn, jnp.where(lane == KQ_NORM_LANE, qn, k_aug))
        qt = jnp.concatenate([q.T, q_aug.T], axis=0).astype(BF16)
        for r in range(tm // Q_TILE):
            qt_ref[0, j, r] = qt[:, r * Q_TILE:(r + 1) * Q_TILE]
        k_ref[0, j, :, 0:HEAD_DIM] = k.astype(BF16)
        k_ref[0, j, :, HEAD_DIM:QK_WIDTH] = k_aug.astype(BF16)
        vt = section(2, j).T.astype(BF16)
        for r in range(tm // KV_TILE):
            vt_ref[0, j, r] = vt[:, r * KV_TILE:(r + 1) * KV_TILE]

        cols = slice(j * GROUP_DIM, (j + 1) * GROUP_DIM)
        u = _gelu(section(3, j))
        sv = _gelu(section(4, j))
        mu = jnp.mean(sv, axis=-1, keepdims=True)
        dv = sv - mu
        var = jnp.mean(dv * dv, axis=-1, keepdims=True)
        vn = (dv * lax.rsqrt(var + LN_EPS) * lng_ref[:, cols] + lnb_ref[:, cols]).astype(BF16)
        w = jnp.where(chunk_causal, ws_ref[j], 0.0).astype(BF16)
        b = bs_ref[j]
        mixed = [
            _dot(w, vn[n * SGU_LEN:(n + 1) * SGU_LEN, :]) + b for n in range(tm // SGU_LEN)
        ]
        mixed = jnp.concatenate(mixed, axis=0)
        ga_ref[:, cols] = jax.nn.sigmoid(section(5, j)).astype(BF16)
        gob_ref[:, cols] = (jax.nn.sigmoid(section(6, j)) * u * mixed).astype(BF16)


def _proj(h, wmain, wf, bf, ln_g, ln_b, w_s, b_s, *, batch, seq, tm):
    n, d = h.shape
    n_c = N_HEADS // HEADS_PER_STEP
    bps = seq // tm
    hp = HEADS_PER_STEP
    return pl.pallas_call(
        functools.partial(_proj_kernel, tm=tm, blocks_per_seq=bps),
        grid=(n // tm, n_c),
        in_specs=[
            pl.BlockSpec((tm, d), lambda i, c: (i, 0)),
            pl.BlockSpec((d, CHUNK_COLS), lambda i, c: (0, c)),
            pl.BlockSpec((d, LANES), lambda i, c: (0, 0)),
            pl.BlockSpec((1, LANES), lambda i, c: (0, 0)),
            pl.BlockSpec((1, hp * GROUP_DIM), lambda i, c: (0, c)),
            pl.BlockSpec((1, hp * GROUP_DIM), lambda i, c: (0, c)),
            pl.BlockSpec((hp, SGU_LEN, SGU_LEN), lambda i, c: (c, 0, 0)),
            pl.BlockSpec((hp, SGU_LEN, 1), lambda i, c: (c, 0, 0)),
        ],
        out_specs=[
            pl.BlockSpec((1, hp, tm // Q_TILE, QK_WIDTH, Q_TILE),
                         lambda i, c: (i // bps, c, i % bps, 0, 0)),
            pl.BlockSpec((1, hp, tm, QK_WIDTH), lambda i, c: (i // bps, c, i % bps, 0)),
            pl.BlockSpec((1, hp, tm // KV_TILE, HEAD_DIM, KV_TILE),
                         lambda i, c: (i // bps, c, i % bps, 0, 0)),
            pl.BlockSpec((tm, hp * GROUP_DIM), lambda i, c: (i, c)),
            pl.BlockSpec((tm, hp * GROUP_DIM), lambda i, c: (i, c)),
        ],
        out_shape=[
            jax.ShapeDtypeStruct((batch, N_HEADS, seq // Q_TILE, QK_WIDTH, Q_TILE), BF16),
            jax.ShapeDtypeStruct((batch, N_HEADS, seq, QK_WIDTH), BF16),
            jax.ShapeDtypeStruct((batch, N_HEADS, seq // KV_TILE, HEAD_DIM, KV_TILE), BF16),
            jax.ShapeDtypeStruct((n, d), BF16),
            jax.ShapeDtypeStruct((n, d), BF16),
        ],
        scratch_shapes=[pltpu.VMEM((tm, LANES), F32), pltpu.VMEM((8, LANES), F32)],
        compiler_params=pltpu.CompilerParams(
            dimension_semantics=("arbitrary", "arbitrary"), vmem_limit_bytes=VMEM_LIMIT_BYTES),
        name="proj",
    )(h, wmain, wf, bf, ln_g, ln_b, w_s, b_s)


def _attn_kernel(qt_ref, k_ref, vt_ref, ga_ref, gob_ref, o_ref, acc_ref, l_ref):
    tk, tq, half = KV_TILE, Q_TILE, Q_TILE // 2
    n_kv = k_ref.shape[2] // tk

    def kv_rows(j):
        return k_ref[0, 0, pl.ds(pl.multiple_of(j * tk, tk), tk), :]

    def causal(s):
        kv_pos = lax.broadcasted_iota(jnp.int32, s.shape, 0)
        q_pos = lax.broadcasted_iota(jnp.int32, s.shape, 1)
        return jnp.where(kv_pos <= q_pos, s, MASK_VALUE)

    def tile_max(j, best):
        return jnp.maximum(best, kv_rows(j)[:, HEAD_DIM:QK_WIDTH].astype(F32))
    best = lax.fori_loop(0, n_kv, tile_max, jnp.zeros((tk, LANES), F32))
    best = jnp.max(best, axis=0, keepdims=True)
    lane = lax.broadcasted_iota(jnp.int32, (1, LANES), 1)
    kn = jnp.max(jnp.where(lane == K_NORM_LANE, best, 0.0), axis=1, keepdims=True)
    qn = jnp.max(jnp.where(lane == KQ_NORM_LANE, best, 0.0), axis=1, keepdims=True)
    safe = 2.0 * jnp.sqrt(jnp.max(kn * qn)) <= SAFE_EXP2_RANGE

    def query_tile(i, _):
        qt = qt_ref.at[0, 0, i]
        qn_row = qt[HEAD_DIM + Q_NORM_LANE:HEAD_DIM + Q_NORM_LANE + 1, :].astype(F32)
        acc_ref[...] = jnp.zeros_like(acc_ref)
        d = 2 * i

        @pl.when(safe)
        def _():
            shift = jnp.sqrt(qn_row * kn) + 1.0

            def tile_pair(t, l):
                kb = k_ref[0, 0, pl.ds(pl.multiple_of(t * tq, tq), tq), :]
                p = jnp.exp2(_dot(kb, qt[...]) - shift)
                vt = jnp.concatenate([vt_ref[0, 0, 2 * t], vt_ref[0, 0, 2 * t + 1]], axis=1)
                acc_ref[...] += _dot(vt, p.astype(BF16))
                return l + jnp.sum(p, axis=0, keepdims=True)

            l = lax.fori_loop(0, i, tile_pair, jnp.zeros((1, tq), F32))
            p = jnp.exp2(causal(_dot(kv_rows(d), qt[...])) - shift)
            acc_ref[...] += _dot(vt_ref[0, 0, d], p.astype(BF16))
            l = l + jnp.sum(p, axis=0, keepdims=True)
            p = jnp.exp2(causal(_dot(kv_rows(d + 1), qt[:, half:])) - shift[:, half:])
            acc_ref[:, half:] += _dot(vt_ref[0, 0, d + 1], p.astype(BF16))
            l_ref[0:1, :] = jnp.concatenate(
                [l[:, :half], l[:, half:] + jnp.sum(p, axis=0, keepdims=True)], axis=1)

        @pl.when(jnp.logical_not(safe))
        def _():
            def online(s, j, carry, cols):
                m, l = carry
                m_new = jnp.maximum(m, jnp.max(s, axis=0, keepdims=True))
                alpha = jnp.exp2(m - m_new)
                p = jnp.exp2(s - m_new)
                acc_ref[:, cols] = alpha * acc_ref[:, cols] + _dot(vt_ref[0, 0, j], p.astype(BF16))
                return m_new, alpha * l + jnp.sum(p, axis=0, keepdims=True)

            every = slice(0, tq)
            init = (jnp.full((1, tq), MASK_VALUE, F32), jnp.zeros((1, tq), F32))
            carry = lax.fori_loop(
                0, d, lambda j, c: online(_dot(kv_rows(j), qt[...]), j, c, every), init)
            m, l = online(causal(_dot(kv_rows(d), qt[...])), d, carry, every)
            upper = slice(half, tq)
            _, l_hi = online(causal(_dot(kv_rows(d + 1), qt[:, half:])), d + 1,
                             (m[:, half:], l[:, half:]), upper)
            l_ref[0:1, :] = jnp.concatenate([l[:, :half], l_hi], axis=1)

        o = (acc_ref[...] / l_ref[0:1, :]).T
        rows = pl.ds(pl.multiple_of(i * tq, tq), tq)
        o_ref[rows, :] = (ga_ref[rows, :].astype(F32) * o + gob_ref[rows, :].astype(F32)).astype(BF16)
        return 0

    lax.fori_loop(0, k_ref.shape[2] // tq, query_tile, 0)


def _attn(qt, k, vt, ga, gob):
    batch, n_heads, seq, _ = k.shape
    tq = Q_TILE
    head_cols = lambda b, h: (b, h)
    whole = lambda b, h: (b, h, 0, 0, 0)
    return pl.pallas_call(
        _attn_kernel,
        grid=(batch, n_heads),
        scratch_shapes=[pltpu.VMEM((HEAD_DIM, tq), F32), pltpu.VMEM((8, tq), F32)],
        in_specs=[
            pl.BlockSpec((1, 1, seq // tq, QK_WIDTH, tq), whole),
            pl.BlockSpec((1, 1, seq, QK_WIDTH), lambda b, h: (b, h, 0, 0)),
            pl.BlockSpec((1, 1, seq // KV_TILE, HEAD_DIM, KV_TILE), whole),
            pl.BlockSpec((seq, HEAD_DIM), head_cols),
            pl.BlockSpec((seq, HEAD_DIM), head_cols),
        ],
        out_specs=pl.BlockSpec((seq, HEAD_DIM), head_cols),
        out_shape=jax.ShapeDtypeStruct(ga.shape, BF16),
        compiler_params=pltpu.CompilerParams(
            dimension_semantics=("parallel", "parallel"), vmem_limit_bytes=VMEM_LIMIT_BYTES),
        name="attn",
    )(qt, k, vt, ga, gob)


def _pack_w_in(w_in):
    d = w_in.shape[0]
    col_f = 3 * D_MODEL
    starts = [0, D_MODEL, 2 * D_MODEL, col_f + N_HEADS, col_f + N_HEADS + D_MODEL,
              col_f + N_HEADS + 2 * D_MODEL, col_f + N_HEADS + 3 * D_MODEL]
    n_c = N_HEADS // HEADS_PER_STEP
    secs = [w_in[:, s:s + D_MODEL].reshape(d, n_c, 1, HEADS_PER_STEP * HEAD_DIM) for s in starts]
    wmain = jnp.concatenate(secs, axis=2).reshape(d, n_c * CHUNK_COLS).astype(BF16)
    wf = w_in[:, col_f:col_f + N_HEADS]
    return wmain, _replicate_forget(wf).astype(BF16)


def _replicate_forget(a):
    rows = a.shape[0]
    rep = jnp.tile(a, (1, C_PARTS))
    gap = jnp.zeros((rows, K_PART_BASE - C_PARTS * N_HEADS), a.dtype)
    tail = jnp.zeros((rows, LANES - K_PART_BASE - C_PARTS * N_HEADS), a.dtype)
    return jnp.concatenate([rep, gap, rep, tail], axis=1)


def kernel(x, ffn1_pre_g, ffn1_w_gate, ffn1_w_up, ffn1_w_down, ffn1_post_g, mix_pre_g, w_in, b_forget, sgu_ln_g, sgu_ln_b, sgu_w_s, sgu_b_s, w_out, mix_post_g, ffn2_pre_g, ffn2_w_gate, ffn2_w_up, ffn2_w_down, ffn2_post_g):
    batch, seq, d = x.shape
    depth = ffn1_pre_g.shape[0]
    tm_ffn = min(FFN_ROWS, seq)
    tm_proj = min(PROJ_ROWS, seq)
    xf = x.reshape(batch * seq, d)
    for l in range(depth):
        row = lambda a: a[l].reshape(1, -1)
        xf, h = _ffn(xf, row(ffn1_pre_g), ffn1_w_gate[l].astype(BF16), ffn1_w_up[l].astype(BF16),
                     ffn1_w_down[l].astype(BF16), row(ffn1_post_g), next_g=row(mix_pre_g),
                     tm=tm_ffn, tf=FFN_CHUNK)
        wmain, wf = _pack_w_in(w_in[l])
        qt, k, vt, ga, gob = _proj(
            h, wmain, wf, _replicate_forget(row(b_forget)),
            row(sgu_ln_g), row(sgu_ln_b), sgu_w_s[l], sgu_b_s[l].reshape(N_HEADS, SGU_LEN, 1),
            batch=batch, seq=seq, tm=tm_proj)
        merged = _attn(qt, k, vt, ga, gob)
        xf = _ffn(xf, row(ffn2_pre_g), ffn2_w_gate[l].astype(BF16), ffn2_w_up[l].astype(BF16),
                  ffn2_w_down[l].astype(BF16), row(ffn2_post_g),
                  mix=(merged, w_out[l].astype(BF16), row(mix_post_g)), tm=tm_ffn, tf=FFN_CHUNK)
    return xf.reshape(batch, seq, d)
```

```python
import functools
import math

import jax
import jax.numpy as jnp
from jax import lax
from jax.experimental import pallas as pl
from jax.experimental.pallas import tpu as pltpu

D_MODEL = 1024
D_FF = 4 * D_MODEL
HEAD_DIM = 128
N_HEADS = D_MODEL // HEAD_DIM
GROUP_DIM = 128
SGU_LEN = 128
CHUNK = 64
RMS_EPS = 1e-6
LN_EPS = 1e-5

LANES = 128
HEADS_PER_STEP = 2
N_SECTIONS = 7
CHUNK_COLS = N_SECTIONS * HEADS_PER_STEP * HEAD_DIM
QK_WIDTH = 2 * HEAD_DIM
C_PARTS = 3
Q_PART_BASE = 0
K_PART_BASE = 32
FFN_ROWS = 512
FFN_CHUNK = 1024
FFN_ROW_GROUPS = 2
PROJ_ROWS = 1024
CUMSUM_ROWS = 256
KV_TILE = 512
Q_TILE = 2 * KV_TILE
Q_NORM_LANE = 64
K_NORM_LANE = 65
KQ_NORM_LANE = 66
NORM_INFLATE = 1.02
MASK_VALUE = -1e30
LOG2E = math.log2(math.e)
SAFE_EXP2_RANGE = 64.0
SKIP_DECAY_LOG2 = 160.0
VMEM_LIMIT_BYTES = 56 * 1024 * 1024

F32 = jnp.float32
BF16 = jnp.bfloat16


def _rms(x, g):
    return x * lax.rsqrt(jnp.mean(x * x, axis=-1, keepdims=True) + RMS_EPS) * g


def _dot(a, b):
    return jnp.dot(a, b, preferred_element_type=F32)


def _gelu(x):
    return 0.5 * x * (1.0 + lax.erf(x * math.sqrt(0.5)))


def _ffn_kernel(*refs, tf, has_mix, has_next):
    refs = list(refs)
    if has_mix:
        m_ref, wo_ref, mpost_ref = refs[:3]
        refs = refs[3:]
    x_ref, pre_ref, wg_ref, wu_ref, wd_ref, post_ref = refs[:6]
    refs = refs[6:]
    if has_next:
        next_g_ref, o_ref, hn_ref = refs
    else:
        (o_ref,) = refs
    tm = x_ref.shape[0]
    groups = 1 if has_mix else FFN_ROW_GROUPS
    for r in range(groups):
        rows = slice(r * tm // groups, (r + 1) * tm // groups)
        x = x_ref[rows, :]
        if has_mix:
            x = x + _rms(_dot(m_ref[rows, :], wo_ref[...]), mpost_ref[...])
        h = _rms(x, pre_ref[...]).astype(BF16)
        y = None
        for f in range(D_FF // tf):
            cols = slice(f * tf, (f + 1) * tf)
            g = _dot(h, wg_ref[:, cols])
            u = _dot(h, wu_ref[:, cols])
            a = (g * jax.nn.sigmoid(g) * u).astype(BF16)
            part = _dot(a, wd_ref[cols, :])
            y = part if y is None else y + part
        out = x + 0.5 * _rms(y, post_ref[...])
        o_ref[rows, :] = out
        if has_next:
            hn_ref[rows, :] = _rms(out, next_g_ref[...]).astype(BF16)


def _ffn(x, pre_g, wg, wu, wd, post_g, mix=None, next_g=None, *, tm, tf):
    n, d = x.shape
    row = lambda i: (i, 0)
    resident = lambda shape: pl.BlockSpec(shape, lambda i: (0, 0), pipeline_mode=pl.Buffered(1))
    in_specs = [
        pl.BlockSpec((tm, d), row),
        resident((1, d)),
        resident((d, D_FF)),
        resident((d, D_FF)),
        resident((D_FF, d)),
        resident((1, d)),
    ]
    args = [x, pre_g, wg, wu, wd, post_g]
    if mix is not None:
        merged, w_out, mix_post_g = mix
        in_specs = [pl.BlockSpec((tm, d), row), resident((d, d)), resident((1, d))] + in_specs
        args = [merged, w_out, mix_post_g] + args
    out_specs = pl.BlockSpec((tm, d), row)
    out_shape = jax.ShapeDtypeStruct((n, d), F32)
    if next_g is not None:
        in_specs.append(resident((1, d)))
        args.append(next_g)
        out_specs = [out_specs, pl.BlockSpec((tm, d), row)]
        out_shape = [out_shape, jax.ShapeDtypeStruct((n, d), BF16)]
    return pl.pallas_call(
        functools.partial(_ffn_kernel, tf=tf, has_mix=mix is not None,
                          has_next=next_g is not None),
        grid=(n // tm,),
        in_specs=in_specs,
        out_specs=out_specs,
        out_shape=out_shape,
        compiler_params=pltpu.CompilerParams(
            dimension_semantics=("parallel",), vmem_limit_bytes=VMEM_LIMIT_BYTES),
        name="ffn_mix" if mix is not None else "ffn",
    )(*args)


def _cumsum_rows(lf, carry):
    rows = lf.shape[0]
    r = lax.broadcasted_iota(jnp.int32, (CUMSUM_ROWS, CUMSUM_ROWS), 0)
    c = lax.broadcasted_iota(jnp.int32, (CUMSUM_ROWS, CUMSUM_ROWS), 1)
    tri = (c <= r).astype(BF16)
    out = []
    for b in range(rows // CUMSUM_ROWS):
        blk = lf[b * CUMSUM_ROWS:(b + 1) * CUMSUM_ROWS]
        t1 = blk.astype(BF16)
        r1 = blk - t1.astype(F32)
        t2 = r1.astype(BF16)
        t3 = (r1 - t2.astype(F32)).astype(BF16)
        cs = (_dot(tri, t1) + _dot(tri, t2)) + _dot(tri, t3) + carry
        carry = cs[CUMSUM_ROWS - 1:CUMSUM_ROWS, :]
        out.append(cs)
    return jnp.concatenate(out, axis=0), carry


def _proj_kernel(h_ref, wmain_ref, wf_ref, bf_ref, lng_ref, lnb_ref, ws_ref, bs_ref,
                 qt_ref, k_ref, vt_ref, ga_ref, gob_ref,
                 cpart_ref, carry_ref, *, tm, blocks_per_seq):
    i = pl.program_id(0)
    c = pl.program_id(1)
    lane = lax.broadcasted_iota(jnp.int32, (tm, LANES), 1)
    in_q_parts = lane < Q_PART_BASE + C_PARTS * N_HEADS
    in_k_parts = (lane >= K_PART_BASE) & (lane < K_PART_BASE + C_PARTS * N_HEADS)

    @pl.when(c == 0)
    def _():
        @pl.when(i % blocks_per_seq == 0)
        def _():
            carry_ref[...] = jnp.zeros_like(carry_ref)

        lf = jax.nn.log_sigmoid(_dot(h_ref[...], wf_ref[...]) + bf_ref[...])
        lf = jnp.where(in_q_parts | in_k_parts, lf, 0.0)
        cs, carry = _cumsum_rows(lf, carry_ref[0:1, :])
        carry_ref[0:1, :] = carry
        cs = cs * LOG2E
        hi = cs.astype(BF16).astype(F32)
        r1 = cs - hi
        mid = r1.astype(BF16).astype(F32)
        lo = r1 - mid
        part = (lane % 32) // N_HEADS
        cpart_ref[...] = jnp.where(part == 0, hi, jnp.where(part == 1, mid, lo))

    z = _dot(h_ref[...], wmain_ref[...])
    sec = HEADS_PER_STEP * HEAD_DIM

    def section(s, j):
        return z[:, s * sec + j * HEAD_DIM: s * sec + (j + 1) * HEAD_DIM]

    cpart = cpart_ref[...]
    scale = LOG2E / math.sqrt(HEAD_DIM)
    wr = lax.broadcasted_iota(jnp.int32, (SGU_LEN, SGU_LEN), 0)
    wc = lax.broadcasted_iota(jnp.int32, (SGU_LEN, SGU_LEN), 1)
    chunk_causal = (wc // CHUNK) <= (wr // CHUNK)
    for j in range(HEADS_PER_STEP):
        head = c * HEADS_PER_STEP + j
        is_head = (lane % N_HEADS) == head
        q = section(0, j) * scale
        k = section(1, j)
        qn = jnp.sum(q * q, axis=-1, keepdims=True) * NORM_INFLATE
        kn = jnp.sum(k * k, axis=-1, keepdims=True) * NORM_INFLATE
        q_aug = jnp.where(in_q_parts, cpart, jnp.where(in_k_parts & is_head, 1.0, 0.0))
        q_aug = jnp.where(lane == Q_NORM_LANE, qn, q_aug)
        k_aug = jnp.where(in_k_parts, -cpart, jnp.where(in_q_parts & is_head, 1.0, 0.0))
        k_aug = jnp.where(lane == K_NORM_LANE, kn, jnp.where(lane == KQ_NORM_LANE, qn, k_aug))
        qt = jnp.concatenate([q.T, q_aug.T], axis=0).astype(BF16)
        for r in range(tm // Q_TILE):
            qt_ref[0, j, r] = qt[:, r * Q_TILE:(r + 1) * Q_TILE]
        k_ref[0, j, :, 0:HEAD_DIM] = k.astype(BF16)
        k_ref[0, j, :, HEAD_DIM:QK_WIDTH] = k_aug.astype(BF16)
        vt = section(2, j).T.astype(BF16)
        for r in range(tm // KV_TILE):
            vt_ref[0, j, r] = vt[:, r * KV_TILE:(r + 1) * KV_TILE]

        cols = slice(j * GROUP_DIM, (j + 1) * GROUP_DIM)
        u = _gelu(section(3, j))
        sv = _gelu(section(4, j))
        mu = jnp.mean(sv, axis=-1, keepdims=True)
        dv = sv - mu
        var = jnp.mean(dv * dv, axis=-1, keepdims=True)
        vn = (dv * lax.rsqrt(var + LN_EPS) * lng_ref[:, cols] + lnb_ref[:, cols]).astype(BF16)
        w = jnp.where(chunk_causal, ws_ref[j], 0.0).astype(BF16)
        b = bs_ref[j]
        mixed = [
            _dot(w, vn[n * SGU_LEN:(n + 1) * SGU_LEN, :]) + b for n in range(tm // SGU_LEN)
        ]
        mixed = jnp.concatenate(mixed, axis=0)
        ga_ref[:, cols] = jax.nn.sigmoid(section(5, j)).astype(BF16)
        gob_ref[:, cols] = (jax.nn.sigmoid(section(6, j)) * u * mixed).astype(BF16)


def _proj(h, wmain, wf, bf, ln_g, ln_b, w_s, b_s, *, batch, seq, tm):
    n, d = h.shape
    n_c = N_HEADS // HEADS_PER_STEP
    bps = seq // tm
    hp = HEADS_PER_STEP
    return pl.pallas_call(
        functools.partial(_proj_kernel, tm=tm, blocks_per_seq=bps),
        grid=(n // tm, n_c),
        in_specs=[
            pl.BlockSpec((tm, d), lambda i, c: (i, 0)),
            pl.BlockSpec((d, CHUNK_COLS), lambda i, c: (0, c)),
            pl.BlockSpec((d, LANES), lambda i, c: (0, 0)),
            pl.BlockSpec((1, LANES), lambda i, c: (0, 0)),
            pl.BlockSpec((1, hp * GROUP_DIM), lambda i, c: (0, c)),
            pl.BlockSpec((1, hp * GROUP_DIM), lambda i, c: (0, c)),
            pl.BlockSpec((hp, SGU_LEN, SGU_LEN), lambda i, c: (c, 0, 0)),
            pl.BlockSpec((hp, SGU_LEN, 1), lambda i, c: (c, 0, 0)),
        ],
        out_specs=[
            pl.BlockSpec((1, hp, tm // Q_TILE, QK_WIDTH, Q_TILE),
                         lambda i, c: (i // bps, c, i % bps, 0, 0)),
            pl.BlockSpec((1, hp, tm, QK_WIDTH), lambda i, c: (i // bps, c, i % bps, 0)),
            pl.BlockSpec((1, hp, tm // KV_TILE, HEAD_DIM, KV_TILE),
                         lambda i, c: (i // bps, c, i % bps, 0, 0)),
            pl.BlockSpec((tm, hp * GROUP_DIM), lambda i, c: (i, c)),
            pl.BlockSpec((tm, hp * GROUP_DIM), lambda i, c: (i, c)),
        ],
        out_shape=[
            jax.ShapeDtypeStruct((batch, N_HEADS, seq // Q_TILE, QK_WIDTH, Q_TILE), BF16),
            jax.ShapeDtypeStruct((batch, N_HEADS, seq, QK_WIDTH), BF16),
            jax.ShapeDtypeStruct((batch, N_HEADS, seq // KV_TILE, HEAD_DIM, KV_TILE), BF16),
            jax.ShapeDtypeStruct((n, d), BF16),
            jax.ShapeDtypeStruct((n, d), BF16),
        ],
        scratch_shapes=[pltpu.VMEM((tm, LANES), F32), pltpu.VMEM((8, LANES), F32)],
        compiler_params=pltpu.CompilerParams(
            dimension_semantics=("arbitrary", "arbitrary"), vmem_limit_bytes=VMEM_LIMIT_BYTES),
        name="proj",
    )(h, wmain, wf, bf, ln_g, ln_b, w_s, b_s)


def _attn_kernel(qt_ref, k_ref, vt_ref, ga_ref, gob_ref, o_ref, acc_ref, l_ref, first_ref):
    tk, tq, half = KV_TILE, Q_TILE, Q_TILE // 2
    n_kv = k_ref.shape[2] // tk

    def kv_rows(j):
        return k_ref[0, 0, pl.ds(pl.multiple_of(j * tk, tk), tk), :]

    def causal(s):
        kv_pos = lax.broadcasted_iota(jnp.int32, s.shape, 0)
        q_pos = lax.broadcasted_iota(jnp.int32, s.shape, 1)
        return jnp.where(kv_pos <= q_pos, s, MASK_VALUE)

    def tile_max(j, best):
        return jnp.maximum(best, kv_rows(j)[:, HEAD_DIM:QK_WIDTH].astype(F32))
    best = lax.fori_loop(0, n_kv, tile_max, jnp.zeros((tk, LANES), F32))
    best = jnp.max(best, axis=0, keepdims=True)
    lane = lax.broadcasted_iota(jnp.int32, (1, LANES), 1)
    kn = jnp.max(jnp.where(lane == K_NORM_LANE, best, 0.0), axis=1, keepdims=True)
    qn = jnp.max(jnp.where(lane == KQ_NORM_LANE, best, 0.0), axis=1, keepdims=True)
    safe = 2.0 * jnp.sqrt(jnp.max(kn * qn)) <= SAFE_EXP2_RANGE

    head = pl.program_id(1)
    mine = (lane >= K_PART_BASE) & (lane < K_PART_BASE + C_PARTS * N_HEADS) & (lane % N_HEADS == head)

    def neg_c(row):
        r = k_ref[0, 0, row:row + 1, HEAD_DIM:QK_WIDTH].astype(F32)
        return jnp.sum(jnp.where(mine, r, 0.0), axis=1, keepdims=True)

    n_q = k_ref.shape[2] // tq
    pair_end = [neg_c((t + 1) * tq - 1) for t in range(n_q - 1)]
    for i in range(n_q):
        q_start = neg_c(i * tq)
        dead = [(q_start - pair_end[t] > SKIP_DECAY_LOG2).astype(jnp.int32) for t in range(i)]
        first_ref[i] = sum(dead)[0, 0] if dead else 0

    def query_tile(i, _):
        qt = qt_ref.at[0, 0, i]
        qn_row = qt[HEAD_DIM + Q_NORM_LANE:HEAD_DIM + Q_NORM_LANE + 1, :].astype(F32)
        acc_ref[...] = jnp.zeros_like(acc_ref)
        d = 2 * i

        @pl.when(safe)
        def _():
            shift = jnp.sqrt(qn_row * kn) + 1.0

            def tile_pair(t, l):
                kb = k_ref[0, 0, pl.ds(pl.multiple_of(t * tq, tq), tq), :]
                p = jnp.exp2(_dot(kb, qt[...]) - shift)
                vt = jnp.concatenate([vt_ref[0, 0, 2 * t], vt_ref[0, 0, 2 * t + 1]], axis=1)
                acc_ref[...] += _dot(vt, p.astype(BF16))
                return l + jnp.sum(p, axis=0, keepdims=True)

            l = lax.fori_loop(first_ref[i], i, tile_pair, jnp.zeros((1, tq), F32))
            p = jnp.exp2(causal(_dot(kv_rows(d), qt[...])) - shift)
            acc_ref[...] += _dot(vt_ref[0, 0, d], p.astype(BF16))
            l = l + jnp.sum(p, axis=0, keepdims=True)
            p = jnp.exp2(causal(_dot(kv_rows(d + 1), qt[:, half:])) - shift[:, half:])
            acc_ref[:, half:] += _dot(vt_ref[0, 0, d + 1], p.astype(BF16))
            l_ref[0:1, :] = jnp.concatenate(
                [l[:, :half], l[:, half:] + jnp.sum(p, axis=0, keepdims=True)], axis=1)

        @pl.when(jnp.logical_not(safe))
        def _():
            def online(s, j, carry, cols):
                m, l = carry
                m_new = jnp.maximum(m, jnp.max(s, axis=0, keepdims=True))
                alpha = jnp.exp2(m - m_new)
                p = jnp.exp2(s - m_new)
                acc_ref[:, cols] = alpha * acc_ref[:, cols] + _dot(vt_ref[0, 0, j], p.astype(BF16))
                return m_new, alpha * l + jnp.sum(p, axis=0, keepdims=True)

            every = slice(0, tq)
            init = (jnp.full((1, tq), MASK_VALUE, F32), jnp.zeros((1, tq), F32))
            carry = lax.fori_loop(
                0, d, lambda j, c: online(_dot(kv_rows(j), qt[...]), j, c, every), init)
            m, l = online(causal(_dot(kv_rows(d), qt[...])), d, carry, every)
            upper = slice(half, tq)
            _, l_hi = online(causal(_dot(kv_rows(d + 1), qt[:, half:])), d + 1,
                             (m[:, half:], l[:, half:]), upper)
            l_ref[0:1, :] = jnp.concatenate([l[:, :half], l_hi], axis=1)

        o = (acc_ref[...] / l_ref[0:1, :]).T
        rows = pl.ds(pl.multiple_of(i * tq, tq), tq)
        o_ref[rows, :] = (ga_ref[rows, :].astype(F32) * o + gob_ref[rows, :].astype(F32)).astype(BF16)
        return 0

    lax.fori_loop(0, k_ref.shape[2] // tq, query_tile, 0)


def _attn(qt, k, vt, ga, gob):
    batch, n_heads, seq, _ = k.shape
    tq = Q_TILE
    head_cols = lambda b, h: (b, h)
    whole = lambda b, h: (b, h, 0, 0, 0)
    return pl.pallas_call(
        _attn_kernel,
        grid=(batch, n_heads),
        scratch_shapes=[pltpu.VMEM((HEAD_DIM, tq), F32), pltpu.VMEM((8, tq), F32),
                        pltpu.SMEM((seq // tq,), jnp.int32)],
        in_specs=[
            pl.BlockSpec((1, 1, seq // tq, QK_WIDTH, tq), whole),
            pl.BlockSpec((1, 1, seq, QK_WIDTH), lambda b, h: (b, h, 0, 0)),
            pl.BlockSpec((1, 1, seq // KV_TILE, HEAD_DIM, KV_TILE), whole),
            pl.BlockSpec((seq, HEAD_DIM), head_cols),
            pl.BlockSpec((seq, HEAD_DIM), head_cols),
        ],
        out_specs=pl.BlockSpec((seq, HEAD_DIM), head_cols),
        out_shape=jax.ShapeDtypeStruct(ga.shape, BF16),
        compiler_params=pltpu.CompilerParams(
            dimension_semantics=("parallel", "parallel"), vmem_limit_bytes=VMEM_LIMIT_BYTES),
        name="attn",
    )(qt, k, vt, ga, gob)


def _pack_w_in(w_in):
    d = w_in.shape[0]
    col_f = 3 * D_MODEL
    starts = [0, D_MODEL, 2 * D_MODEL, col_f + N_HEADS, col_f + N_HEADS + D_MODEL,
              col_f + N_HEADS + 2 * D_MODEL, col_f + N_HEADS + 3 * D_MODEL]
    n_c = N_HEADS // HEADS_PER_STEP
    secs = [w_in[:, s:s + D_MODEL].reshape(d, n_c, 1, HEADS_PER_STEP * HEAD_DIM) for s in starts]
    wmain = jnp.concatenate(secs, axis=2).reshape(d, n_c * CHUNK_COLS).astype(BF16)
    wf = w_in[:, col_f:col_f + N_HEADS]
    return wmain, _replicate_forget(wf).astype(BF16)


def _replicate_forget(a):
    rows = a.shape[0]
    rep = jnp.tile(a, (1, C_PARTS))
    gap = jnp.zeros((rows, K_PART_BASE - C_PARTS * N_HEADS), a.dtype)
    tail = jnp.zeros((rows, LANES - K_PART_BASE - C_PARTS * N_HEADS), a.dtype)
    return jnp.concatenate([rep, gap, rep, tail], axis=1)


def kernel(x, ffn1_pre_g, ffn1_w_gate, ffn1_w_up, ffn1_w_down, ffn1_post_g, mix_pre_g, w_in, b_forget, sgu_ln_g, sgu_ln_b, sgu_w_s, sgu_b_s, w_out, mix_post_g, ffn2_pre_g, ffn2_w_gate, ffn2_w_up, ffn2_w_down, ffn2_post_g):
    batch, seq, d = x.shape
    depth = ffn1_pre_g.shape[0]
    tm_ffn = min(FFN_ROWS, seq)
    tm_proj = min(PROJ_ROWS, seq)
    xf = x.reshape(batch * seq, d)
    for l in range(depth):
        row = lambda a: a[l].reshape(1, -1)
        xf, h = _ffn(xf, row(ffn1_pre_g), ffn1_w_gate[l].astype(BF16), ffn1_w_up[l].astype(BF16),
                     ffn1_w_down[l].astype(BF16), row(ffn1_post_g), next_g=row(mix_pre_g),
                     tm=tm_ffn, tf=FFN_CHUNK)
        wmain, wf = _pack_w_in(w_in[l])
        qt, k, vt, ga, gob = _proj(
            h, wmain, wf, _replicate_forget(row(b_forget)),
            row(sgu_ln_g), row(sgu_ln_b), sgu_w_s[l], sgu_b_s[l].reshape(N_HEADS, SGU_LEN, 1),
            batch=batch, seq=seq, tm=tm_proj)
        merged = _attn(qt, k, vt, ga, gob)
        xf = _ffn(xf, row(ffn2_pre_g), ffn2_w_gate[l].astype(BF16), ffn2_w_up[l].astype(BF16),
                  ffn2_w_down[l].astype(BF16), row(ffn2_post_g),
                  mix=(merged, w_out[l].astype(BF16), row(mix_post_g)), tm=tm_ffn, tf=FFN_CHUNK)
    return xf.reshape(batch, seq, d)
```

```python
import functools
import math

import jax
import jax.numpy as jnp
from jax import lax
from jax.experimental import pallas as pl
from jax.experimental.pallas import tpu as pltpu

D_MODEL = 1024
D_FF = 4 * D_MODEL
HEAD_DIM = 128
N_HEADS = D_MODEL // HEAD_DIM
GROUP_DIM = 128
SGU_LEN = 128
CHUNK = 64
RMS_EPS = 1e-6
LN_EPS = 1e-5

LANES = 128
HEADS_PER_STEP = 2
N_SECTIONS = 7
SEC_Q, SEC_K, SEC_V, SEC_U, SEC_SV, SEC_GA, SEC_GB = range(N_SECTIONS)
CHUNK_COLS = N_SECTIONS * HEADS_PER_STEP * HEAD_DIM
QK_WIDTH = 2 * HEAD_DIM
C_PARTS = 3
Q_PART_BASE = 0
K_PART_BASE = 32
FFN_ROWS = 512
FFN_CHUNK = 1024
FFN_ROW_GROUPS = 2
PROJ_ROWS = 1024
CUMSUM_ROWS = 256
KV_TILE = 512
Q_TILE = 2 * KV_TILE
Q_NORM_LANE = 64
K_NORM_LANE = 65
KQ_NORM_LANE = 66
NORM_INFLATE = 1.02
MASK_VALUE = -1e30
LOG2E = math.log2(math.e)
SAFE_EXP2_RANGE = 64.0
SKIP_DECAY_LOG2 = 160.0
VMEM_LIMIT_BYTES = 56 * 1024 * 1024

F32 = jnp.float32
BF16 = jnp.bfloat16


def _rms(x, g):
    return x * lax.rsqrt(jnp.mean(x * x, axis=-1, keepdims=True) + RMS_EPS) * g


def _dot(a, b):
    return jnp.dot(a, b, preferred_element_type=F32)


def _gelu(x):
    return 0.5 * x * (1.0 + lax.erf(x * math.sqrt(0.5)))


def _ffn_kernel(*refs, tf, has_mix, has_next):
    refs = list(refs)
    if has_mix:
        m_ref, wo_ref, mpost_ref = refs[:3]
        refs = refs[3:]
    x_ref, pre_ref, wg_ref, wu_ref, wd_ref, post_ref = refs[:6]
    refs = refs[6:]
    if has_next:
        next_g_ref, o_ref, hn_ref = refs
    else:
        (o_ref,) = refs
    tm = x_ref.shape[0]
    groups = 1 if has_mix else FFN_ROW_GROUPS
    for r in range(groups):
        rows = slice(r * tm // groups, (r + 1) * tm // groups)
        x = x_ref[rows, :]
        if has_mix:
            x = x + _rms(_dot(m_ref[rows, :], wo_ref[...]), mpost_ref[...])
        h = _rms(x, pre_ref[...]).astype(BF16)
        y = None
        for f in range(D_FF // tf):
            cols = slice(f * tf, (f + 1) * tf)
            g = _dot(h, wg_ref[:, cols])
            u = _dot(h, wu_ref[:, cols])
            a = (g * jax.nn.sigmoid(g) * u).astype(BF16)
            part = _dot(a, wd_ref[cols, :])
            y = part if y is None else y + part
        out = x + 0.5 * _rms(y, post_ref[...])
        o_ref[rows, :] = out
        if has_next:
            hn_ref[rows, :] = _rms(out, next_g_ref[...]).astype(BF16)


def _ffn(x, pre_g, wg, wu, wd, post_g, mix=None, next_g=None, *, tm, tf):
    n, d = x.shape
    row = lambda i: (i, 0)
    resident = lambda shape: pl.BlockSpec(shape, lambda i: (0, 0), pipeline_mode=pl.Buffered(1))
    in_specs = [
        pl.BlockSpec((tm, d), row),
        resident((1, d)),
        resident((d, D_FF)),
        resident((d, D_FF)),
        resident((D_FF, d)),
        resident((1, d)),
    ]
    args = [x, pre_g, wg, wu, wd, post_g]
    if mix is not None:
        merged, w_out, mix_post_g = mix
        in_specs = [pl.BlockSpec((tm, d), row), resident((d, d)), resident((1, d))] + in_specs
        args = [merged, w_out, mix_post_g] + args
    out_specs = pl.BlockSpec((tm, d), row)
    out_shape = jax.ShapeDtypeStruct((n, d), F32)
    if next_g is not None:
        in_specs.append(resident((1, d)))
        args.append(next_g)
        out_specs = [out_specs, pl.BlockSpec((tm, d), row)]
        out_shape = [out_shape, jax.ShapeDtypeStruct((n, d), BF16)]
    return pl.pallas_call(
        functools.partial(_ffn_kernel, tf=tf, has_mix=mix is not None,
                          has_next=next_g is not None),
        grid=(n // tm,),
        in_specs=in_specs,
        out_specs=out_specs,
        out_shape=out_shape,
        compiler_params=pltpu.CompilerParams(
            dimension_semantics=("parallel",), vmem_limit_bytes=VMEM_LIMIT_BYTES),
        name="ffn_mix" if mix is not None else "ffn",
    )(*args)


def _cumsum_rows(lf, carry):
    rows = lf.shape[0]
    r = lax.broadcasted_iota(jnp.int32, (CUMSUM_ROWS, CUMSUM_ROWS), 0)
    c = lax.broadcasted_iota(jnp.int32, (CUMSUM_ROWS, CUMSUM_ROWS), 1)
    tri = (c <= r).astype(BF16)
    t1 = lf.astype(BF16)
    r1 = lf - t1.astype(F32)
    t2 = r1.astype(BF16)
    t3 = (r1 - t2.astype(F32)).astype(BF16)
    n_blk = rows // CUMSUM_ROWS
    wide = jnp.concatenate(
        [t[b * CUMSUM_ROWS:(b + 1) * CUMSUM_ROWS] for t in (t1, t2, t3) for b in range(n_blk)],
        axis=1)
    local = _dot(tri, wide)
    out = []
    for b in range(n_blk):
        parts = [local[:, (p * n_blk + b) * LANES:(p * n_blk + b + 1) * LANES] for p in range(3)]
        cs = (parts[0] + parts[1]) + parts[2] + carry
        carry = cs[CUMSUM_ROWS - 1:CUMSUM_ROWS, :]
        out.append(cs)
    return jnp.concatenate(out, axis=0), carry


def _proj_kernel(h_ref, wmain_ref, wf_ref, bf_ref, lng_ref, lnb_ref, ws_ref, bs_ref,
                 qt_ref, k_ref, vt_ref, ga_ref, gob_ref,
                 cpart_ref, carry_ref, *, tm, blocks_per_seq):
    i = pl.program_id(0)
    c = pl.program_id(1)
    lane = lax.broadcasted_iota(jnp.int32, (tm, LANES), 1)
    in_q_parts = lane < Q_PART_BASE + C_PARTS * N_HEADS
    in_k_parts = (lane >= K_PART_BASE) & (lane < K_PART_BASE + C_PARTS * N_HEADS)

    @pl.when(c == 0)
    def _():
        @pl.when(i % blocks_per_seq == 0)
        def _():
            carry_ref[...] = jnp.zeros_like(carry_ref)

        lf = jax.nn.log_sigmoid(_dot(h_ref[...], wf_ref[...]) + bf_ref[...])
        lf = jnp.where(in_q_parts | in_k_parts, lf, 0.0)
        cs, carry = _cumsum_rows(lf, carry_ref[0:1, :])
        carry_ref[0:1, :] = carry
        cs = cs * LOG2E
        hi = cs.astype(BF16).astype(F32)
        r1 = cs - hi
        mid = r1.astype(BF16).astype(F32)
        lo = r1 - mid
        part = (lane % 32) // N_HEADS
        cpart_ref[...] = jnp.where(part == 0, hi, jnp.where(part == 1, mid, lo))

    z = _dot(h_ref[...], wmain_ref[...])
    sec = HEADS_PER_STEP * HEAD_DIM

    def section(s, j):
        return z[:, s * sec + j * HEAD_DIM: s * sec + (j + 1) * HEAD_DIM]

    cpart = cpart_ref[...]
    scale = LOG2E / math.sqrt(HEAD_DIM)
    wr = lax.broadcasted_iota(jnp.int32, (SGU_LEN, SGU_LEN), 0)
    wc = lax.broadcasted_iota(jnp.int32, (SGU_LEN, SGU_LEN), 1)
    chunk_causal = (wc // CHUNK) <= (wr // CHUNK)
    for j in range(HEADS_PER_STEP):
        head = c * HEADS_PER_STEP + j
        is_head = (lane % N_HEADS) == head
        q = section(SEC_Q, j) * scale
        k = section(SEC_K, j)
        qn = jnp.sum(q * q, axis=-1, keepdims=True) * NORM_INFLATE
        kn = jnp.sum(k * k, axis=-1, keepdims=True) * NORM_INFLATE
        q_aug = jnp.where(in_q_parts, cpart, jnp.where(in_k_parts & is_head, 1.0, 0.0))
        q_aug = jnp.where(lane == Q_NORM_LANE, qn, q_aug)
        k_aug = jnp.where(in_k_parts, -cpart, jnp.where(in_q_parts & is_head, 1.0, 0.0))
        k_aug = jnp.where(lane == K_NORM_LANE, kn, jnp.where(lane == KQ_NORM_LANE, qn, k_aug))
        qt = jnp.concatenate([q.T, q_aug.T], axis=0).astype(BF16)
        for r in range(tm // Q_TILE):
            qt_ref[0, j, r] = qt[:, r * Q_TILE:(r + 1) * Q_TILE]
        k_ref[0, j, :, 0:HEAD_DIM] = k.astype(BF16)
        k_ref[0, j, :, HEAD_DIM:QK_WIDTH] = k_aug.astype(BF16)
        vt = section(SEC_V, j).T.astype(BF16)
        for r in range(tm // KV_TILE):
            vt_ref[0, j, r] = vt[:, r * KV_TILE:(r + 1) * KV_TILE]

        cols = slice(j * GROUP_DIM, (j + 1) * GROUP_DIM)
        u = _gelu(section(SEC_U, j))
        sv = _gelu(section(SEC_SV, j))
        mu = jnp.mean(sv, axis=-1, keepdims=True)
        dv = sv - mu
        var = jnp.mean(dv * dv, axis=-1, keepdims=True)
        vn = (dv * lax.rsqrt(var + LN_EPS) * lng_ref[:, cols] + lnb_ref[:, cols]).astype(BF16)
        w = jnp.where(chunk_causal, ws_ref[j], 0.0).astype(BF16)
        b = bs_ref[j]
        mixed = [
            _dot(w, vn[n * SGU_LEN:(n + 1) * SGU_LEN, :]) + b for n in range(tm // SGU_LEN)
        ]
        mixed = jnp.concatenate(mixed, axis=0)
        ga_ref[:, cols] = jax.nn.sigmoid(section(SEC_GA, j)).astype(BF16)
        gob_ref[:, cols] = (jax.nn.sigmoid(section(SEC_GB, j)) * u * mixed).astype(BF16)


def _proj(h, wmain, wf, bf, ln_g, ln_b, w_s, b_s, *, batch, seq, tm):
    n, d = h.shape
    n_c = N_HEADS // HEADS_PER_STEP
    bps = seq // tm
    hp = HEADS_PER_STEP
    return pl.pallas_call(
        functools.partial(_proj_kernel, tm=tm, blocks_per_seq=bps),
        grid=(n // tm, n_c),
        in_specs=[
            pl.BlockSpec((tm, d), lambda i, c: (i, 0)),
            pl.BlockSpec((d, CHUNK_COLS), lambda i, c: (0, c)),
            pl.BlockSpec((d, LANES), lambda i, c: (0, 0)),
            pl.BlockSpec((1, LANES), lambda i, c: (0, 0)),
            pl.BlockSpec((1, hp * GROUP_DIM), lambda i, c: (0, c)),
            pl.BlockSpec((1, hp * GROUP_DIM), lambda i, c: (0, c)),
            pl.BlockSpec((hp, SGU_LEN, SGU_LEN), lambda i, c: (c, 0, 0)),
            pl.BlockSpec((hp, SGU_LEN, 1), lambda i, c: (c, 0, 0)),
        ],
        out_specs=[
            pl.BlockSpec((1, hp, tm // Q_TILE, QK_WIDTH, Q_TILE),
                         lambda i, c: (i // bps, c, i % bps, 0, 0)),
            pl.BlockSpec((1, hp, tm, QK_WIDTH), lambda i, c: (i // bps, c, i % bps, 0)),
            pl.BlockSpec((1, hp, tm // KV_TILE, HEAD_DIM, KV_TILE),
                         lambda i, c: (i // bps, c, i % bps, 0, 0)),
            pl.BlockSpec((tm, hp * GROUP_DIM), lambda i, c: (i, c)),
            pl.BlockSpec((tm, hp * GROUP_DIM), lambda i, c: (i, c)),
        ],
        out_shape=[
            jax.ShapeDtypeStruct((batch, N_HEADS, seq // Q_TILE, QK_WIDTH, Q_TILE), BF16),
            jax.ShapeDtypeStruct((batch, N_HEADS, seq, QK_WIDTH), BF16),
            jax.ShapeDtypeStruct((batch, N_HEADS, seq // KV_TILE, HEAD_DIM, KV_TILE), BF16),
            jax.ShapeDtypeStruct((n, d), BF16),
            jax.ShapeDtypeStruct((n, d), BF16),
        ],
        scratch_shapes=[pltpu.VMEM((tm, LANES), F32), pltpu.VMEM((8, LANES), F32)],
        compiler_params=pltpu.CompilerParams(
            dimension_semantics=("arbitrary", "arbitrary"), vmem_limit_bytes=VMEM_LIMIT_BYTES),
        name="proj",
    )(h, wmain, wf, bf, ln_g, ln_b, w_s, b_s)


def _attn_kernel(qt_ref, k_ref, vt_ref, ga_ref, gob_ref, o_ref, acc_ref, l_ref, first_ref):
    tk, tq, half = KV_TILE, Q_TILE, Q_TILE // 2
    n_kv = k_ref.shape[2] // tk

    def kv_rows(j):
        return k_ref[0, 0, pl.ds(pl.multiple_of(j * tk, tk), tk), :]

    def causal(s):
        kv_pos = lax.broadcasted_iota(jnp.int32, s.shape, 0)
        q_pos = lax.broadcasted_iota(jnp.int32, s.shape, 1)
        return jnp.where(kv_pos <= q_pos, s, MASK_VALUE)

    def tile_max(j, best):
        return jnp.maximum(best, kv_rows(j)[:, HEAD_DIM:QK_WIDTH].astype(F32))
    best = lax.fori_loop(0, n_kv, tile_max, jnp.zeros((tk, LANES), F32))
    best = jnp.max(best, axis=0, keepdims=True)
    lane = lax.broadcasted_iota(jnp.int32, (1, LANES), 1)
    kn = jnp.max(jnp.where(lane == K_NORM_LANE, best, 0.0), axis=1, keepdims=True)
    qn = jnp.max(jnp.where(lane == KQ_NORM_LANE, best, 0.0), axis=1, keepdims=True)
    safe = 2.0 * jnp.sqrt(jnp.max(kn * qn)) <= SAFE_EXP2_RANGE

    head = pl.program_id(1)
    mine = (lane >= K_PART_BASE) & (lane < K_PART_BASE + C_PARTS * N_HEADS) & (lane % N_HEADS == head)

    def neg_c(row):
        r = k_ref[0, 0, row:row + 1, HEAD_DIM:QK_WIDTH].astype(F32)
        return jnp.sum(jnp.where(mine, r, 0.0), axis=1, keepdims=True)

    n_q = k_ref.shape[2] // tq
    tile_end = [neg_c((j + 1) * tk - 1) for j in range(2 * (n_q - 1))]
    for i in range(n_q):
        q_start = neg_c(i * tq)
        dead = [(q_start - tile_end[j] > SKIP_DECAY_LOG2).astype(jnp.int32) for j in range(2 * i)]
        first_ref[i] = sum(dead)[0, 0] if dead else 0

    def query_tile(i, _):
        qt = qt_ref.at[0, 0, i]
        qn_row = qt[HEAD_DIM + Q_NORM_LANE:HEAD_DIM + Q_NORM_LANE + 1, :].astype(F32)
        acc_ref[...] = jnp.zeros_like(acc_ref)
        d = 2 * i

        @pl.when(safe)
        def _():
            shift = jnp.sqrt(qn_row * kn) + 1.0

            def tile_pair(t, l):
                kb = k_ref[0, 0, pl.ds(pl.multiple_of(t * tq, tq), tq), :]
                p = jnp.exp2(_dot(kb, qt[...]) - shift)
                vt = jnp.concatenate([vt_ref[0, 0, 2 * t], vt_ref[0, 0, 2 * t + 1]], axis=1)
                acc_ref[...] += _dot(vt, p.astype(BF16))
                return l + jnp.sum(p, axis=0, keepdims=True)

            first = first_ref[i]
            l_ref[0:1, :] = jnp.zeros((1, tq), F32)

            @pl.when((first & 1) == 1)
            def _():
                p = jnp.exp2(_dot(kv_rows(first), qt[...]) - shift)
                acc_ref[...] += _dot(vt_ref[0, 0, first], p.astype(BF16))
                l_ref[0:1, :] = jnp.sum(p, axis=0, keepdims=True)

            l = lax.fori_loop(lax.shift_right_logical(first + 1, 1), i, tile_pair, l_ref[0:1, :])
            p = jnp.exp2(causal(_dot(kv_rows(d), qt[...])) - shift)
            acc_ref[...] += _dot(vt_ref[0, 0, d], p.astype(BF16))
            l = l + jnp.sum(p, axis=0, keepdims=True)
            p = jnp.exp2(causal(_dot(kv_rows(d + 1), qt[:, half:])) - shift[:, half:])
            acc_ref[:, half:] += _dot(vt_ref[0, 0, d + 1], p.astype(BF16))
            l_ref[0:1, :] = jnp.concatenate(
                [l[:, :half], l[:, half:] + jnp.sum(p, axis=0, keepdims=True)], axis=1)

        @pl.when(jnp.logical_not(safe))
        def _():
            def online(s, j, carry, cols):
                m, l = carry
                m_new = jnp.maximum(m, jnp.max(s, axis=0, keepdims=True))
                alpha = jnp.exp2(m - m_new)
                p = jnp.exp2(s - m_new)
                acc_ref[:, cols] = alpha * acc_ref[:, cols] + _dot(vt_ref[0, 0, j], p.astype(BF16))
                return m_new, alpha * l + jnp.sum(p, axis=0, keepdims=True)

            every = slice(0, tq)
            init = (jnp.full((1, tq), MASK_VALUE, F32), jnp.zeros((1, tq), F32))
            carry = lax.fori_loop(
                0, d, lambda j, c: online(_dot(kv_rows(j), qt[...]), j, c, every), init)
            m, l = online(causal(_dot(kv_rows(d), qt[...])), d, carry, every)
            upper = slice(half, tq)
            _, l_hi = online(causal(_dot(kv_rows(d + 1), qt[:, half:])), d + 1,
                             (m[:, half:], l[:, half:]), upper)
            l_ref[0:1, :] = jnp.concatenate([l[:, :half], l_hi], axis=1)

        o = (acc_ref[...] / l_ref[0:1, :]).T
        rows = pl.ds(pl.multiple_of(i * tq, tq), tq)
        o_ref[rows, :] = (ga_ref[rows, :].astype(F32) * o + gob_ref[rows, :].astype(F32)).astype(BF16)
        return 0

    lax.fori_loop(0, k_ref.shape[2] // tq, query_tile, 0)


def _attn(qt, k, vt, ga, gob):
    batch, n_heads, seq, _ = k.shape
    tq = Q_TILE
    head_cols = lambda b, h: (b, h)
    whole = lambda b, h: (b, h, 0, 0, 0)
    return pl.pallas_call(
        _attn_kernel,
        grid=(batch, n_heads),
        scratch_shapes=[pltpu.VMEM((HEAD_DIM, tq), F32), pltpu.VMEM((8, tq), F32),
                        pltpu.SMEM((seq // tq,), jnp.int32)],
        in_specs=[
            pl.BlockSpec((1, 1, seq // tq, QK_WIDTH, tq), whole),
            pl.BlockSpec((1, 1, seq, QK_WIDTH), lambda b, h: (b, h, 0, 0)),
            pl.BlockSpec((1, 1, seq // KV_TILE, HEAD_DIM, KV_TILE), whole),
            pl.BlockSpec((seq, HEAD_DIM), head_cols),
            pl.BlockSpec((seq, HEAD_DIM), head_cols),
        ],
        out_specs=pl.BlockSpec((seq, HEAD_DIM), head_cols),
        out_shape=jax.ShapeDtypeStruct(ga.shape, BF16),
        compiler_params=pltpu.CompilerParams(
            dimension_semantics=("parallel", "parallel"), vmem_limit_bytes=VMEM_LIMIT_BYTES),
        name="attn",
    )(qt, k, vt, ga, gob)


def _pack_w_in(w_in):
    d = w_in.shape[0]
    col_f = 3 * D_MODEL
    first_col = {SEC_Q: 0, SEC_K: D_MODEL, SEC_V: 2 * D_MODEL, SEC_U: col_f + N_HEADS,
                 SEC_SV: col_f + N_HEADS + D_MODEL, SEC_GA: col_f + N_HEADS + 2 * D_MODEL,
                 SEC_GB: col_f + N_HEADS + 3 * D_MODEL}
    starts = [first_col[sec] for sec in range(N_SECTIONS)]
    n_c = N_HEADS // HEADS_PER_STEP
    secs = [w_in[:, s:s + D_MODEL].reshape(d, n_c, 1, HEADS_PER_STEP * HEAD_DIM) for s in starts]
    wmain = jnp.concatenate(secs, axis=2).reshape(d, n_c * CHUNK_COLS).astype(BF16)
    wf = w_in[:, col_f:col_f + N_HEADS]
    return wmain, _replicate_forget(wf).astype(BF16)


def _replicate_forget(a):
    rows = a.shape[0]
    rep = jnp.tile(a, (1, C_PARTS))
    gap = jnp.zeros((rows, K_PART_BASE - C_PARTS * N_HEADS), a.dtype)
    tail = jnp.zeros((rows, LANES - K_PART_BASE - C_PARTS * N_HEADS), a.dtype)
    return jnp.concatenate([rep, gap, rep, tail], axis=1)


def kernel(x, ffn1_pre_g, ffn1_w_gate, ffn1_w_up, ffn1_w_down, ffn1_post_g, mix_pre_g, w_in, b_forget, sgu_ln_g, sgu_ln_b, sgu_w_s, sgu_b_s, w_out, mix_post_g, ffn2_pre_g, ffn2_w_gate, ffn2_w_up, ffn2_w_down, ffn2_post_g):
    batch, seq, d = x.shape
    depth = ffn1_pre_g.shape[0]
    tm_ffn = min(FFN_ROWS, seq)
    tm_proj = min(PROJ_ROWS, seq)
    xf = x.reshape(batch * seq, d)
    for l in range(depth):
        row = lambda a: a[l].reshape(1, -1)
        xf, h = _ffn(xf, row(ffn1_pre_g), ffn1_w_gate[l].astype(BF16), ffn1_w_up[l].astype(BF16),
                     ffn1_w_down[l].astype(BF16), row(ffn1_post_g), next_g=row(mix_pre_g),
                     tm=tm_ffn, tf=FFN_CHUNK)
        wmain, wf = _pack_w_in(w_in[l])
        qt, k, vt, ga, gob = _proj(
            h, wmain, wf, _replicate_forget(row(b_forget)),
            row(sgu_ln_g), row(sgu_ln_b), sgu_w_s[l], sgu_b_s[l].reshape(N_HEADS, SGU_LEN, 1),
            batch=batch, seq=seq, tm=tm_proj)
        merged = _attn(qt, k, vt, ga, gob)
        xf = _ffn(xf, row(ffn2_pre_g), ffn2_w_gate[l].astype(BF16), ffn2_w_up[l].astype(BF16),
                  ffn2_w_down[l].astype(BF16), row(ffn2_post_g),
                  mix=(merged, w_out[l].astype(BF16), row(mix_post_g)), tm=tm_ffn, tf=FFN_CHUNK)
    return xf.reshape(batch, seq, d)
```

```python
import functools
import math

import jax
import jax.numpy as jnp
from jax import lax
from jax.experimental import pallas as pl
from jax.experimental.pallas import tpu as pltpu

D_MODEL = 1024
D_FF = 4 * D_MODEL
HEAD_DIM = 128
N_HEADS = D_MODEL // HEAD_DIM
GROUP_DIM = 128
SGU_LEN = 128
CHUNK = 64
RMS_EPS = 1e-6
LN_EPS = 1e-5

LANES = 128
HEADS_PER_STEP = 2
N_SECTIONS = 7
SEC_Q, SEC_K, SEC_V, SEC_U, SEC_SV, SEC_GA, SEC_GB = range(N_SECTIONS)
CHUNK_COLS = N_SECTIONS * HEADS_PER_STEP * HEAD_DIM
QK_WIDTH = 2 * HEAD_DIM
C_PARTS = 3
Q_PART_BASE = 0
K_PART_BASE = 32
FFN_ROWS = 1024
FFN_CHUNK = 1024
FFN_ROWS_PLAIN = 1024
FFN_GROUP_ROWS = 256
PROJ_ROWS = 1024
CUMSUM_ROWS = 256
KV_TILE = 512
Q_TILE = 2 * KV_TILE
Q_NORM_LANE = 64
K_NORM_LANE = 65
KQ_NORM_LANE = 66
NORM_INFLATE = 1.02
MASK_VALUE = -1e30
LOG2E = math.log2(math.e)
SAFE_EXP2_RANGE = 64.0
SKIP_DECAY_LOG2 = 160.0
VMEM_LIMIT_BYTES = 56 * 1024 * 1024

F32 = jnp.float32
BF16 = jnp.bfloat16


def _rms(x, g):
    return x * lax.rsqrt(jnp.mean(x * x, axis=-1, keepdims=True) + RMS_EPS) * g


def _dot(a, b):
    return jnp.dot(a, b, preferred_element_type=F32)


def _gelu(x):
    return 0.5 * x * (1.0 + lax.erf(x * math.sqrt(0.5)))


def _ffn_kernel(*refs, tf, has_mix, has_next):
    refs = list(refs)
    if has_mix:
        m_ref, wo_ref, mpost_ref = refs[:3]
        refs = refs[3:]
    x_ref, pre_ref, wg_ref, wu_ref, wd_ref, post_ref = refs[:6]
    refs = refs[6:]
    if has_next:
        next_g_ref, o_ref, hn_ref = refs
    else:
        (o_ref,) = refs
    tm = x_ref.shape[0]
    groups = max(tm // FFN_GROUP_ROWS, 1)
    x_all = x_ref[...]
    if has_mix:
        x_all = x_all + _rms(_dot(m_ref[...], wo_ref[...]), mpost_ref[...])
    for r in range(groups):
        rows = slice(r * tm // groups, (r + 1) * tm // groups)
        x = x_all[rows, :]
        h = _rms(x, pre_ref[...]).astype(BF16)
        y = None
        for f in range(D_FF // tf):
            cols = slice(f * tf, (f + 1) * tf)
            g = _dot(h, wg_ref[:, cols])
            u = _dot(h, wu_ref[:, cols])
            a = (g * jax.nn.sigmoid(g) * u).astype(BF16)
            part = _dot(a, wd_ref[cols, :])
            y = part if y is None else y + part
        out = x + 0.5 * _rms(y, post_ref[...])
        o_ref[rows, :] = out
        if has_next:
            hn_ref[rows, :] = _rms(out, next_g_ref[...]).astype(BF16)


def _ffn(x, pre_g, wg, wu, wd, post_g, mix=None, next_g=None, *, tm, tf):
    n, d = x.shape
    row = lambda i: (i, 0)
    resident = lambda shape: pl.BlockSpec(shape, lambda i: (0, 0), pipeline_mode=pl.Buffered(1))
    in_specs = [
        pl.BlockSpec((tm, d), row),
        resident((1, d)),
        resident((d, D_FF)),
        resident((d, D_FF)),
        resident((D_FF, d)),
        resident((1, d)),
    ]
    args = [x, pre_g, wg, wu, wd, post_g]
    if mix is not None:
        merged, w_out, mix_post_g = mix
        in_specs = [pl.BlockSpec((tm, d), row), resident((d, d)), resident((1, d))] + in_specs
        args = [merged, w_out, mix_post_g] + args
    out_specs = pl.BlockSpec((tm, d), row)
    out_shape = jax.ShapeDtypeStruct((n, d), F32)
    if next_g is not None:
        in_specs.append(resident((1, d)))
        args.append(next_g)
        out_specs = [out_specs, pl.BlockSpec((tm, d), row)]
        out_shape = [out_shape, jax.ShapeDtypeStruct((n, d), BF16)]
    return pl.pallas_call(
        functools.partial(_ffn_kernel, tf=tf, has_mix=mix is not None,
                          has_next=next_g is not None),
        grid=(n // tm,),
        in_specs=in_specs,
        out_specs=out_specs,
        out_shape=out_shape,
        compiler_params=pltpu.CompilerParams(
            dimension_semantics=("parallel",), vmem_limit_bytes=VMEM_LIMIT_BYTES),
        name="ffn_mix" if mix is not None else "ffn",
    )(*args)


def _cumsum_rows(lf, carry):
    rows = lf.shape[0]
    r = lax.broadcasted_iota(jnp.int32, (CUMSUM_ROWS, CUMSUM_ROWS), 0)
    c = lax.broadcasted_iota(jnp.int32, (CUMSUM_ROWS, CUMSUM_ROWS), 1)
    tri = (c <= r).astype(BF16)
    t1 = lf.astype(BF16)
    r1 = lf - t1.astype(F32)
    t2 = r1.astype(BF16)
    t3 = (r1 - t2.astype(F32)).astype(BF16)
    n_blk = rows // CUMSUM_ROWS
    wide = jnp.concatenate(
        [t[b * CUMSUM_ROWS:(b + 1) * CUMSUM_ROWS] for t in (t1, t2, t3) for b in range(n_blk)],
        axis=1)
    local = _dot(tri, wide)
    out = []
    for b in range(n_blk):
        parts = [local[:, (p * n_blk + b) * LANES:(p * n_blk + b + 1) * LANES] for p in range(3)]
        cs = (parts[0] + parts[1]) + parts[2] + carry
        carry = cs[CUMSUM_ROWS - 1:CUMSUM_ROWS, :]
        out.append(cs)
    return jnp.concatenate(out, axis=0), carry


def _proj_kernel(h_ref, wmain_ref, wf_ref, bf_ref, lng_ref, lnb_ref, ws_ref, bs_ref,
                 qt_ref, k_ref, vt_ref, ga_ref, gob_ref,
                 cpart_ref, carry_ref, *, tm, blocks_per_seq):
    i = pl.program_id(0)
    c = pl.program_id(1)
    lane = lax.broadcasted_iota(jnp.int32, (tm, LANES), 1)
    in_q_parts = lane < Q_PART_BASE + C_PARTS * N_HEADS
    in_k_parts = (lane >= K_PART_BASE) & (lane < K_PART_BASE + C_PARTS * N_HEADS)

    @pl.when(c == 0)
    def _():
        @pl.when(i % blocks_per_seq == 0)
        def _():
            carry_ref[...] = jnp.zeros_like(carry_ref)

        lf = jax.nn.log_sigmoid(_dot(h_ref[...], wf_ref[...]) + bf_ref[...])
        lf = jnp.where(in_q_parts | in_k_parts, lf, 0.0)
        cs, carry = _cumsum_rows(lf, carry_ref[0:1, :])
        carry_ref[0:1, :] = carry
        cs = cs * LOG2E
        hi = cs.astype(BF16).astype(F32)
        r1 = cs - hi
        mid = r1.astype(BF16).astype(F32)
        lo = r1 - mid
        part = (lane % 32) // N_HEADS
        cpart_ref[...] = jnp.where(part == 0, hi, jnp.where(part == 1, mid, lo))

    z = _dot(h_ref[...], wmain_ref[...])
    sec = HEADS_PER_STEP * HEAD_DIM

    def section(s, j):
        return z[:, s * sec + j * HEAD_DIM: s * sec + (j + 1) * HEAD_DIM]

    cpart = cpart_ref[...]
    scale = LOG2E / math.sqrt(HEAD_DIM)
    wr = lax.broadcasted_iota(jnp.int32, (SGU_LEN, SGU_LEN), 0)
    wc = lax.broadcasted_iota(jnp.int32, (SGU_LEN, SGU_LEN), 1)
    chunk_causal = (wc // CHUNK) <= (wr // CHUNK)
    for j in range(HEADS_PER_STEP):
        head = c * HEADS_PER_STEP + j
        is_head = (lane % N_HEADS) == head
        q = section(SEC_Q, j) * scale
        k = section(SEC_K, j)
        qn = jnp.sum(q * q, axis=-1, keepdims=True) * NORM_INFLATE
        kn = jnp.sum(k * k, axis=-1, keepdims=True) * NORM_INFLATE
        q_aug = jnp.where(in_q_parts, cpart, jnp.where(in_k_parts & is_head, 1.0, 0.0))
        q_aug = jnp.where(lane == Q_NORM_LANE, qn, q_aug)
        k_aug = jnp.where(in_k_parts, -cpart, jnp.where(in_q_parts & is_head, 1.0, 0.0))
        k_aug = jnp.where(lane == K_NORM_LANE, kn, jnp.where(lane == KQ_NORM_LANE, qn, k_aug))
        qt = jnp.concatenate([q.T, q_aug.T], axis=0).astype(BF16)
        for r in range(tm // Q_TILE):
            qt_ref[0, j, r] = qt[:, r * Q_TILE:(r + 1) * Q_TILE]
        k_ref[0, j, :, 0:HEAD_DIM] = k.astype(BF16)
        k_ref[0, j, :, HEAD_DIM:QK_WIDTH] = k_aug.astype(BF16)
        vt = section(SEC_V, j).T.astype(BF16)
        for r in range(tm // KV_TILE):
            vt_ref[0, j, r] = vt[:, r * KV_TILE:(r + 1) * KV_TILE]

        cols = slice(j * GROUP_DIM, (j + 1) * GROUP_DIM)
        u = _gelu(section(SEC_U, j))
        sv = _gelu(section(SEC_SV, j))
        mu = jnp.mean(sv, axis=-1, keepdims=True)
        dv = sv - mu
        var = jnp.mean(dv * dv, axis=-1, keepdims=True)
        vn = (dv * lax.rsqrt(var + LN_EPS) * lng_ref[:, cols] + lnb_ref[:, cols]).astype(BF16)
        w = jnp.where(chunk_causal, ws_ref[j], 0.0).astype(BF16)
        b = bs_ref[j]
        mixed = [
            _dot(w, vn[n * SGU_LEN:(n + 1) * SGU_LEN, :]) + b for n in range(tm // SGU_LEN)
        ]
        mixed = jnp.concatenate(mixed, axis=0)
        ga_ref[:, cols] = jax.nn.sigmoid(section(SEC_GA, j)).astype(BF16)
        gob_ref[:, cols] = (jax.nn.sigmoid(section(SEC_GB, j)) * u * mixed).astype(BF16)


def _proj(h, wmain, wf, bf, ln_g, ln_b, w_s, b_s, *, batch, seq, tm):
    n, d = h.shape
    n_c = N_HEADS // HEADS_PER_STEP
    bps = seq // tm
    hp = HEADS_PER_STEP
    return pl.pallas_call(
        functools.partial(_proj_kernel, tm=tm, blocks_per_seq=bps),
        grid=(n // tm, n_c),
        in_specs=[
            pl.BlockSpec((tm, d), lambda i, c: (i, 0)),
            pl.BlockSpec((d, CHUNK_COLS), lambda i, c: (0, c)),
            pl.BlockSpec((d, LANES), lambda i, c: (0, 0)),
            pl.BlockSpec((1, LANES), lambda i, c: (0, 0)),
            pl.BlockSpec((1, hp * GROUP_DIM), lambda i, c: (0, c)),
            pl.BlockSpec((1, hp * GROUP_DIM), lambda i, c: (0, c)),
            pl.BlockSpec((hp, SGU_LEN, SGU_LEN), lambda i, c: (c, 0, 0)),
            pl.BlockSpec((hp, SGU_LEN, 1), lambda i, c: (c, 0, 0)),
        ],
        out_specs=[
            pl.BlockSpec((1, hp, tm // Q_TILE, QK_WIDTH, Q_TILE),
                         lambda i, c: (i // bps, c, i % bps, 0, 0)),
            pl.BlockSpec((1, hp, tm, QK_WIDTH), lambda i, c: (i // bps, c, i % bps, 0)),
            pl.BlockSpec((1, hp, tm // KV_TILE, HEAD_DIM, KV_TILE),
                         lambda i, c: (i // bps, c, i % bps, 0, 0)),
            pl.BlockSpec((tm, hp * GROUP_DIM), lambda i, c: (i, c)),
            pl.BlockSpec((tm, hp * GROUP_DIM), lambda i, c: (i, c)),
        ],
        out_shape=[
            jax.ShapeDtypeStruct((batch, N_HEADS, seq // Q_TILE, QK_WIDTH, Q_TILE), BF16),
            jax.ShapeDtypeStruct((batch, N_HEADS, seq, QK_WIDTH), BF16),
            jax.ShapeDtypeStruct((batch, N_HEADS, seq // KV_TILE, HEAD_DIM, KV_TILE), BF16),
            jax.ShapeDtypeStruct((n, d), BF16),
            jax.ShapeDtypeStruct((n, d), BF16),
        ],
        scratch_shapes=[pltpu.VMEM((tm, LANES), F32), pltpu.VMEM((8, LANES), F32)],
        compiler_params=pltpu.CompilerParams(
            dimension_semantics=("arbitrary", "arbitrary"), vmem_limit_bytes=VMEM_LIMIT_BYTES),
        name="proj",
    )(h, wmain, wf, bf, ln_g, ln_b, w_s, b_s)


def _attn_kernel(qt_ref, k_ref, vt_ref, ga_ref, gob_ref, o_ref, acc_ref, l_ref, first_ref):
    tk, tq, half = KV_TILE, Q_TILE, Q_TILE // 2
    n_kv = k_ref.shape[2] // tk

    def kv_rows(j):
        return k_ref[0, 0, pl.ds(pl.multiple_of(j * tk, tk), tk), :]

    def causal(s):
        kv_pos = lax.broadcasted_iota(jnp.int32, s.shape, 0)
        q_pos = lax.broadcasted_iota(jnp.int32, s.shape, 1)
        return jnp.where(kv_pos <= q_pos, s, MASK_VALUE)

    def tile_max(j, best):
        return jnp.maximum(best, kv_rows(j)[:, HEAD_DIM:QK_WIDTH].astype(F32))
    best = lax.fori_loop(0, n_kv, tile_max, jnp.zeros((tk, LANES), F32))
    best = jnp.max(best, axis=0, keepdims=True)
    lane = lax.broadcasted_iota(jnp.int32, (1, LANES), 1)
    kn = jnp.max(jnp.where(lane == K_NORM_LANE, best, 0.0), axis=1, keepdims=True)
    qn = jnp.max(jnp.where(lane == KQ_NORM_LANE, best, 0.0), axis=1, keepdims=True)
    safe = 2.0 * jnp.sqrt(jnp.max(kn * qn)) <= SAFE_EXP2_RANGE

    head = pl.program_id(1)
    mine = (lane >= K_PART_BASE) & (lane < K_PART_BASE + C_PARTS * N_HEADS) & (lane % N_HEADS == head)

    def neg_c(row):
        r = k_ref[0, 0, row:row + 1, HEAD_DIM:QK_WIDTH].astype(F32)
        return jnp.sum(jnp.where(mine, r, 0.0), axis=1, keepdims=True)

    n_q = k_ref.shape[2] // tq
    tile_end = [neg_c((j + 1) * tk - 1) for j in range(2 * (n_q - 1))]
    for i in range(n_q):
        q_start = neg_c(i * tq)
        dead = [(q_start - tile_end[j] > SKIP_DECAY_LOG2).astype(jnp.int32) for j in range(2 * i)]
        first_ref[i] = sum(dead)[0, 0] if dead else 0

    def query_tile(i, _):
        qt = qt_ref.at[0, 0, i]
        qn_row = qt[HEAD_DIM + Q_NORM_LANE:HEAD_DIM + Q_NORM_LANE + 1, :].astype(F32)
        acc_ref[...] = jnp.zeros_like(acc_ref)
        d = 2 * i

        @pl.when(safe)
        def _():
            shift = jnp.sqrt(qn_row * kn) + 1.0

            def tile_pair(t, l):
                kb = k_ref[0, 0, pl.ds(pl.multiple_of(t * tq, tq), tq), :]
                p = jnp.exp2(_dot(kb, qt[...]) - shift)
                vt = jnp.concatenate([vt_ref[0, 0, 2 * t], vt_ref[0, 0, 2 * t + 1]], axis=1)
                acc_ref[...] += _dot(vt, p.astype(BF16))
                return l + jnp.sum(p, axis=0, keepdims=True)

            first = first_ref[i]
            l_ref[0:1, :] = jnp.zeros((1, tq), F32)

            @pl.when((first & 1) == 1)
            def _():
                p = jnp.exp2(_dot(kv_rows(first), qt[...]) - shift)
                acc_ref[...] += _dot(vt_ref[0, 0, first], p.astype(BF16))
                l_ref[0:1, :] = jnp.sum(p, axis=0, keepdims=True)

            l = lax.fori_loop(lax.shift_right_logical(first + 1, 1), i, tile_pair, l_ref[0:1, :])
            p = jnp.exp2(causal(_dot(kv_rows(d), qt[...])) - shift)
            acc_ref[...] += _dot(vt_ref[0, 0, d], p.astype(BF16))
            l = l + jnp.sum(p, axis=0, keepdims=True)
            p = jnp.exp2(causal(_dot(kv_rows(d + 1), qt[:, half:])) - shift[:, half:])
            acc_ref[:, half:] += _dot(vt_ref[0, 0, d + 1], p.astype(BF16))
            l_ref[0:1, :] = jnp.concatenate(
                [l[:, :half], l[:, half:] + jnp.sum(p, axis=0, keepdims=True)], axis=1)

        @pl.when(jnp.logical_not(safe))
        def _():
            def online(s, j, carry, cols):
                m, l = carry
                m_new = jnp.maximum(m, jnp.max(s, axis=0, keepdims=True))
                alpha = jnp.exp2(m - m_new)
                p = jnp.exp2(s - m_new)
                acc_ref[:, cols] = alpha * acc_ref[:, cols] + _dot(vt_ref[0, 0, j], p.astype(BF16))
                return m_new, alpha * l + jnp.sum(p, axis=0, keepdims=True)

            every = slice(0, tq)
            init = (jnp.full((1, tq), MASK_VALUE, F32), jnp.zeros((1, tq), F32))
            carry = lax.fori_loop(
                0, d, lambda j, c: online(_dot(kv_rows(j), qt[...]), j, c, every), init)
            m, l = online(causal(_dot(kv_rows(d), qt[...])), d, carry, every)
            upper = slice(half, tq)
            _, l_hi = online(causal(_dot(kv_rows(d + 1), qt[:, half:])), d + 1,
                             (m[:, half:], l[:, half:]), upper)
            l_ref[0:1, :] = jnp.concatenate([l[:, :half], l_hi], axis=1)

        o = (acc_ref[...] / l_ref[0:1, :]).T
        rows = pl.ds(pl.multiple_of(i * tq, tq), tq)
        o_ref[rows, :] = (ga_ref[rows, :].astype(F32) * o + gob_ref[rows, :].astype(F32)).astype(BF16)
        return 0

    lax.fori_loop(0, k_ref.shape[2] // tq, query_tile, 0)


def _attn(qt, k, vt, ga, gob):
    batch, n_heads, seq, _ = k.shape
    tq = Q_TILE
    head_cols = lambda b, h: (b, h)
    whole = lambda b, h: (b, h, 0, 0, 0)
    return pl.pallas_call(
        _attn_kernel,
        grid=(batch, n_heads),
        scratch_shapes=[pltpu.VMEM((HEAD_DIM, tq), F32), pltpu.VMEM((8, tq), F32),
                        pltpu.SMEM((seq // tq,), jnp.int32)],
        in_specs=[
            pl.BlockSpec((1, 1, seq // tq, QK_WIDTH, tq), whole),
            pl.BlockSpec((1, 1, seq, QK_WIDTH), lambda b, h: (b, h, 0, 0)),
            pl.BlockSpec((1, 1, seq // KV_TILE, HEAD_DIM, KV_TILE), whole),
            pl.BlockSpec((seq, HEAD_DIM), head_cols),
            pl.BlockSpec((seq, HEAD_DIM), head_cols),
        ],
        out_specs=pl.BlockSpec((seq, HEAD_DIM), head_cols),
        out_shape=jax.ShapeDtypeStruct(ga.shape, BF16),
        compiler_params=pltpu.CompilerParams(
            dimension_semantics=("parallel", "parallel"), vmem_limit_bytes=VMEM_LIMIT_BYTES),
        name="attn",
    )(qt, k, vt, ga, gob)


def _pack_w_in(w_in):
    d = w_in.shape[0]
    col_f = 3 * D_MODEL
    first_col = {SEC_Q: 0, SEC_K: D_MODEL, SEC_V: 2 * D_MODEL, SEC_U: col_f + N_HEADS,
                 SEC_SV: col_f + N_HEADS + D_MODEL, SEC_GA: col_f + N_HEADS + 2 * D_MODEL,
                 SEC_GB: col_f + N_HEADS + 3 * D_MODEL}
    starts = [first_col[sec] for sec in range(N_SECTIONS)]
    n_c = N_HEADS // HEADS_PER_STEP
    secs = [w_in[:, s:s + D_MODEL].reshape(d, n_c, 1, HEADS_PER_STEP * HEAD_DIM) for s in starts]
    wmain = jnp.concatenate(secs, axis=2).reshape(d, n_c * CHUNK_COLS).astype(BF16)
    wf = w_in[:, col_f:col_f + N_HEADS]
    return wmain, _replicate_forget(wf).astype(BF16)


def _replicate_forget(a):
    rows = a.shape[0]
    rep = jnp.tile(a, (1, C_PARTS))
    gap = jnp.zeros((rows, K_PART_BASE - C_PARTS * N_HEADS), a.dtype)
    tail = jnp.zeros((rows, LANES - K_PART_BASE - C_PARTS * N_HEADS), a.dtype)
    return jnp.concatenate([rep, gap, rep, tail], axis=1)


def kernel(x, ffn1_pre_g, ffn1_w_gate, ffn1_w_up, ffn1_w_down, ffn1_post_g, mix_pre_g, w_in, b_forget, sgu_ln_g, sgu_ln_b, sgu_w_s, sgu_b_s, w_out, mix_post_g, ffn2_pre_g, ffn2_w_gate, ffn2_w_up, ffn2_w_down, ffn2_post_g):
    batch, seq, d = x.shape
    depth = ffn1_pre_g.shape[0]
    tm_ffn = min(FFN_ROWS, seq)
    tm_proj = min(PROJ_ROWS, seq)
    xf = x.reshape(batch * seq, d)
    for l in range(depth):
        row = lambda a: a[l].reshape(1, -1)
        xf, h = _ffn(xf, row(ffn1_pre_g), ffn1_w_gate[l].astype(BF16), ffn1_w_up[l].astype(BF16),
                     ffn1_w_down[l].astype(BF16), row(ffn1_post_g), next_g=row(mix_pre_g),
                     tm=min(FFN_ROWS_PLAIN, seq), tf=FFN_CHUNK)
        wmain, wf = _pack_w_in(w_in[l])
        qt, k, vt, ga, gob = _proj(
            h, wmain, wf, _replicate_forget(row(b_forget)),
            row(sgu_ln_g), row(sgu_ln_b), sgu_w_s[l], sgu_b_s[l].reshape(N_HEADS, SGU_LEN, 1),
            batch=batch, seq=seq, tm=tm_proj)
        merged = _attn(qt, k, vt, ga, gob)
        xf = _ffn(xf, row(ffn2_pre_g), ffn2_w_gate[l].astype(BF16), ffn2_w_up[l].astype(BF16),
                  ffn2_w_down[l].astype(BF16), row(ffn2_post_g),
                  mix=(merged, w_out[l].astype(BF16), row(mix_post_g)), tm=tm_ffn, tf=FFN_CHUNK)
    return xf.reshape(batch, seq, d)
```

```python
import functools
import math

import jax
import jax.numpy as jnp
from jax import lax
from jax.experimental import pallas as pl
from jax.experimental.pallas import tpu as pltpu

D_MODEL = 1024
D_FF = 4 * D_MODEL
HEAD_DIM = 128
N_HEADS = D_MODEL // HEAD_DIM
GROUP_DIM = 128
SGU_LEN = 128
CHUNK = 64
RMS_EPS = 1e-6
LN_EPS = 1e-5

LANES = 128
BF16_ROWS = 16
HEADS_PER_STEP = 2
N_SECTIONS = 7
SEC_Q, SEC_K, SEC_V, SEC_U, SEC_SV, SEC_GA, SEC_GB = range(N_SECTIONS)
CHUNK_COLS = N_SECTIONS * HEADS_PER_STEP * HEAD_DIM
QK_WIDTH = 2 * HEAD_DIM
C_PARTS = 3
Q_PART_BASE = 0
K_PART_BASE = 32
FFN_ROWS = 1024
FFN_CHUNK = 1024
FFN_ROWS_PLAIN = 512
FFN_GROUP_ROWS = 256
PROJ_ROWS = 1024
CUMSUM_ROWS = 256
KV_TILE = 512
Q_TILE = 2 * KV_TILE
Q_NORM_LANE = 64
K_NORM_LANE = 65
KQ_NORM_LANE = 66
NORM_INFLATE = 1.02
MASK_VALUE = -1e30
LOG2E = math.log2(math.e)
SAFE_EXP2_RANGE = 64.0
SKIP_DECAY_LOG2 = 160.0
VMEM_LIMIT_BYTES = 56 * 1024 * 1024

F32 = jnp.float32
BF16 = jnp.bfloat16


def _rms(x, g):
    return x * lax.rsqrt(jnp.mean(x * x, axis=-1, keepdims=True) + RMS_EPS) * g


def _dot(a, b):
    return jnp.dot(a, b, preferred_element_type=F32)


def _gelu(x):
    return 0.5 * x * (1.0 + lax.erf(x * math.sqrt(0.5)))


def _ffn_kernel(*refs, tf, has_mix, has_next):
    refs = list(refs)
    if has_mix:
        m_ref, wo_ref, mpost_ref = refs[:3]
        refs = refs[3:]
    x_ref, pre_ref, wg_ref, wu_ref, wd_ref, post_ref = refs[:6]
    refs = refs[6:]
    if has_next:
        next_g_ref, o_ref, hn_ref = refs
    else:
        (o_ref,) = refs
    tm = x_ref.shape[0]
    groups = max(tm // FFN_GROUP_ROWS, 1)
    x_all = x_ref[...]
    if has_mix:
        x_all = x_all + _rms(_dot(m_ref[...], wo_ref[...]), mpost_ref[...])
    for r in range(groups):
        rows = slice(r * tm // groups, (r + 1) * tm // groups)
        x = x_all[rows, :]
        h = _rms(x, pre_ref[...]).astype(BF16)
        y = None
        for f in range(D_FF // tf):
            cols = slice(f * tf, (f + 1) * tf)
            g = _dot(h, wg_ref[:, cols])
            u = _dot(h, wu_ref[:, cols])
            a = (g * jax.nn.sigmoid(g) * u).astype(BF16)
            part = _dot(a, wd_ref[cols, :])
            y = part if y is None else y + part
        out = x + 0.5 * _rms(y, post_ref[...])
        o_ref[rows, :] = out
        if has_next:
            hn_ref[rows, :] = _rms(out, next_g_ref[...]).astype(BF16)


def _ffn(x, pre_g, wg, wu, wd, post_g, mix=None, next_g=None, *, tm, tf):
    n, d = x.shape
    row = lambda i: (i, 0)
    resident = lambda shape: pl.BlockSpec(shape, lambda i: (0, 0), pipeline_mode=pl.Buffered(1))
    in_specs = [
        pl.BlockSpec((tm, d), row),
        resident((1, d)),
        resident((d, D_FF)),
        resident((d, D_FF)),
        resident((D_FF, d)),
        resident((1, d)),
    ]
    args = [x, pre_g, wg, wu, wd, post_g]
    if mix is not None:
        merged, w_out, mix_post_g = mix
        in_specs = [pl.BlockSpec((tm, d), row), resident((d, d)), resident((1, d))] + in_specs
        args = [merged, w_out, mix_post_g] + args
    out_specs = pl.BlockSpec((tm, d), row)
    out_shape = jax.ShapeDtypeStruct((n, d), F32)
    if next_g is not None:
        in_specs.append(resident((1, d)))
        args.append(next_g)
        out_specs = [out_specs, pl.BlockSpec((tm, d), row)]
        out_shape = [out_shape, jax.ShapeDtypeStruct((n, d), BF16)]
    return pl.pallas_call(
        functools.partial(_ffn_kernel, tf=tf, has_mix=mix is not None,
                          has_next=next_g is not None),
        grid=(n // tm,),
        in_specs=in_specs,
        out_specs=out_specs,
        out_shape=out_shape,
        compiler_params=pltpu.CompilerParams(
            dimension_semantics=("parallel",), vmem_limit_bytes=VMEM_LIMIT_BYTES),
        name="ffn_mix" if mix is not None else "ffn",
    )(*args)


def _cumsum_rows(lf, carry):
    rows = lf.shape[0]
    r = lax.broadcasted_iota(jnp.int32, (CUMSUM_ROWS, CUMSUM_ROWS), 0)
    c = lax.broadcasted_iota(jnp.int32, (CUMSUM_ROWS, CUMSUM_ROWS), 1)
    tri = (c <= r).astype(BF16)
    t1 = lf.astype(BF16)
    r1 = lf - t1.astype(F32)
    t2 = r1.astype(BF16)
    t3 = (r1 - t2.astype(F32)).astype(BF16)
    n_blk = rows // CUMSUM_ROWS
    wide = jnp.concatenate(
        [t[b * CUMSUM_ROWS:(b + 1) * CUMSUM_ROWS] for t in (t1, t2, t3) for b in range(n_blk)],
        axis=1)
    local = _dot(tri, wide)
    out = []
    for b in range(n_blk):
        parts = [local[:, (p * n_blk + b) * LANES:(p * n_blk + b + 1) * LANES] for p in range(3)]
        cs = (parts[0] + parts[1]) + parts[2] + carry
        carry = cs[CUMSUM_ROWS - 1:CUMSUM_ROWS, :]
        out.append(cs)
    return jnp.concatenate(out, axis=0), carry


def _proj_kernel(h_ref, wmain_ref, wf_ref, bf_ref, lng_ref, lnb_ref, ws_ref, bs_ref,
                 qt_ref, k_ref, vt_ref, ga_ref, gob_ref,
                 cpart_ref, carry_ref, *, tm, blocks_per_seq):
    i = pl.program_id(0)
    c = pl.program_id(1)
    lane = lax.broadcasted_iota(jnp.int32, (tm, LANES), 1)
    in_q_parts = lane < Q_PART_BASE + C_PARTS * N_HEADS
    in_k_parts = (lane >= K_PART_BASE) & (lane < K_PART_BASE + C_PARTS * N_HEADS)

    @pl.when(c == 0)
    def _():
        @pl.when(i % blocks_per_seq == 0)
        def _():
            carry_ref[...] = jnp.zeros_like(carry_ref)

        lf = jax.nn.log_sigmoid(_dot(h_ref[...], wf_ref[...]) + bf_ref[...])
        lf = jnp.where(in_q_parts | in_k_parts, lf, 0.0)
        cs, carry = _cumsum_rows(lf, carry_ref[0:1, :])
        carry_ref[0:1, :] = carry
        cs = cs * LOG2E
        hi = cs.astype(BF16).astype(F32)
        r1 = cs - hi
        mid = r1.astype(BF16).astype(F32)
        lo = r1 - mid
        part = (lane % 32) // N_HEADS
        cpart_ref[...] = jnp.where(part == 0, hi, jnp.where(part == 1, mid, lo))

    z = _dot(h_ref[...], wmain_ref[...])
    sec = HEADS_PER_STEP * HEAD_DIM

    def section(s, j):
        return z[:, s * sec + j * HEAD_DIM: s * sec + (j + 1) * HEAD_DIM]

    cpart = cpart_ref[...]
    scale = LOG2E / math.sqrt(HEAD_DIM)
    wr = lax.broadcasted_iota(jnp.int32, (SGU_LEN, SGU_LEN), 0)
    wc = lax.broadcasted_iota(jnp.int32, (SGU_LEN, SGU_LEN), 1)
    chunk_causal = (wc // CHUNK) <= (wr // CHUNK)
    for j in range(HEADS_PER_STEP):
        head = c * HEADS_PER_STEP + j
        is_head = (lane % N_HEADS) == head
        q = section(SEC_Q, j) * scale
        k = section(SEC_K, j)
        qn = jnp.sum(q * q, axis=-1, keepdims=True) * NORM_INFLATE
        kn = jnp.sum(k * k, axis=-1, keepdims=True) * NORM_INFLATE
        q_aug = jnp.where(in_q_parts, cpart, jnp.where(in_k_parts & is_head, 1.0, 0.0))
        q_aug = jnp.where(lane == Q_NORM_LANE, qn, q_aug)
        k_aug = jnp.where(in_k_parts, -cpart, jnp.where(in_q_parts & is_head, 1.0, 0.0))
        k_aug = jnp.where(lane == K_NORM_LANE, kn, jnp.where(lane == KQ_NORM_LANE, qn, k_aug))
        qt = jnp.concatenate([q.T, q_aug.T], axis=0).astype(BF16)
        for r in range(tm // Q_TILE):
            qt_ref[0, j, r] = qt[:, r * Q_TILE:(r + 1) * Q_TILE]
        k_ref[0, j, :, 0:HEAD_DIM] = k.astype(BF16)
        k_ref[0, j, :, HEAD_DIM:QK_WIDTH] = k_aug.astype(BF16)
        vt = section(SEC_V, j).T.astype(BF16)
        for r in range(tm // KV_TILE):
            vt_ref[0, j, r] = vt[:, r * KV_TILE:(r + 1) * KV_TILE]

        cols = slice(j * GROUP_DIM, (j + 1) * GROUP_DIM)
        u = _gelu(section(SEC_U, j))
        sv = _gelu(section(SEC_SV, j))
        mu = jnp.mean(sv, axis=-1, keepdims=True)
        dv = sv - mu
        var = jnp.mean(dv * dv, axis=-1, keepdims=True)
        vn = (dv * lax.rsqrt(var + LN_EPS) * lng_ref[:, cols] + lnb_ref[:, cols]).astype(BF16)
        w = jnp.where(chunk_causal, ws_ref[j], 0.0).astype(BF16)
        b = bs_ref[j]
        mixed = [
            _dot(w, vn[n * SGU_LEN:(n + 1) * SGU_LEN, :]) + b for n in range(tm // SGU_LEN)
        ]
        mixed = jnp.concatenate(mixed, axis=0)
        ga_ref[:, cols] = jax.nn.sigmoid(section(SEC_GA, j)).astype(BF16)
        gob_ref[:, cols] = (jax.nn.sigmoid(section(SEC_GB, j)) * u * mixed).astype(BF16)


def _proj(h, wmain, wf, bf, ln_g, ln_b, w_s, b_s, *, batch, seq, tm):
    n, d = h.shape
    n_c = N_HEADS // HEADS_PER_STEP
    bps = seq // tm
    hp = HEADS_PER_STEP
    return pl.pallas_call(
        functools.partial(_proj_kernel, tm=tm, blocks_per_seq=bps),
        grid=(n // tm, n_c),
        in_specs=[
            pl.BlockSpec((tm, d), lambda i, c: (i, 0)),
            pl.BlockSpec((d, CHUNK_COLS), lambda i, c: (0, c)),
            pl.BlockSpec((d, LANES), lambda i, c: (0, 0)),
            pl.BlockSpec((1, LANES), lambda i, c: (0, 0)),
            pl.BlockSpec((1, hp * GROUP_DIM), lambda i, c: (0, c)),
            pl.BlockSpec((1, hp * GROUP_DIM), lambda i, c: (0, c)),
            pl.BlockSpec((hp, SGU_LEN, SGU_LEN), lambda i, c: (c, 0, 0)),
            pl.BlockSpec((hp, SGU_LEN, 1), lambda i, c: (c, 0, 0)),
        ],
        out_specs=[
            pl.BlockSpec((1, hp, tm // Q_TILE, QK_WIDTH, Q_TILE),
                         lambda i, c: (i // bps, c, i % bps, 0, 0)),
            pl.BlockSpec((1, hp, tm, QK_WIDTH), lambda i, c: (i // bps, c, i % bps, 0)),
            pl.BlockSpec((1, hp, tm // KV_TILE, HEAD_DIM, KV_TILE),
                         lambda i, c: (i // bps, c, i % bps, 0, 0)),
            pl.BlockSpec((tm, hp * GROUP_DIM), lambda i, c: (i, c)),
            pl.BlockSpec((tm, hp * GROUP_DIM), lambda i, c: (i, c)),
        ],
        out_shape=[
            jax.ShapeDtypeStruct((batch, N_HEADS, seq // Q_TILE, QK_WIDTH, Q_TILE), BF16),
            jax.ShapeDtypeStruct((batch, N_HEADS, seq, QK_WIDTH), BF16),
            jax.ShapeDtypeStruct((batch, N_HEADS, seq // KV_TILE, HEAD_DIM, KV_TILE), BF16),
            jax.ShapeDtypeStruct((n, d), BF16),
            jax.ShapeDtypeStruct((n, d), BF16),
        ],
        scratch_shapes=[pltpu.VMEM((tm, LANES), F32), pltpu.VMEM((8, LANES), F32)],
        compiler_params=pltpu.CompilerParams(
            dimension_semantics=("arbitrary", "arbitrary"), vmem_limit_bytes=VMEM_LIMIT_BYTES),
        name="proj",
    )(h, wmain, wf, bf, ln_g, ln_b, w_s, b_s)


def _attn_kernel(qt_ref, k_ref, vt_ref, ga_ref, gob_ref, o_ref, acc_ref, l_ref, first_ref):
    tk, tq, half = KV_TILE, Q_TILE, Q_TILE // 2
    n_kv = k_ref.shape[2] // tk

    def kv_rows(j):
        return k_ref[0, 0, pl.ds(pl.multiple_of(j * tk, tk), tk), :]

    def causal(s):
        kv_pos = lax.broadcasted_iota(jnp.int32, s.shape, 0)
        q_pos = lax.broadcasted_iota(jnp.int32, s.shape, 1)
        return jnp.where(kv_pos <= q_pos, s, MASK_VALUE)

    def tile_max(j, best):
        rows = kv_rows(j)[:, HEAD_DIM:QK_WIDTH].reshape(tk // BF16_ROWS, BF16_ROWS, LANES)
        return jnp.maximum(best, jnp.max(rows, axis=0))
    best = lax.fori_loop(0, n_kv, tile_max, jnp.zeros((BF16_ROWS, LANES), BF16))
    best = jnp.max(best.astype(F32), axis=0, keepdims=True)
    lane = lax.broadcasted_iota(jnp.int32, (1, LANES), 1)
    kn = jnp.max(jnp.where(lane == K_NORM_LANE, best, 0.0), axis=1, keepdims=True)
    qn = jnp.max(jnp.where(lane == KQ_NORM_LANE, best, 0.0), axis=1, keepdims=True)
    safe = 2.0 * jnp.sqrt(jnp.max(kn * qn)) <= SAFE_EXP2_RANGE

    head = pl.program_id(1)
    mine = (lane >= K_PART_BASE) & (lane < K_PART_BASE + C_PARTS * N_HEADS) & (lane % N_HEADS == head)

    def neg_c(row):
        r = k_ref[0, 0, row:row + 1, HEAD_DIM:QK_WIDTH].astype(F32)
        return jnp.sum(jnp.where(mine, r, 0.0), axis=1, keepdims=True)

    n_q = k_ref.shape[2] // tq
    tile_end = [neg_c((j + 1) * tk - 1) for j in range(2 * (n_q - 1))]
    for i in range(n_q):
        q_start = neg_c(i * tq)
        dead = [(q_start - tile_end[j] > SKIP_DECAY_LOG2).astype(jnp.int32) for j in range(2 * i)]
        first_ref[i] = sum(dead)[0, 0] if dead else 0

    def query_tile(i, _):
        qt = qt_ref.at[0, 0, i]
        qn_row = qt[HEAD_DIM + Q_NORM_LANE:HEAD_DIM + Q_NORM_LANE + 1, :].astype(F32)
        acc_ref[...] = jnp.zeros_like(acc_ref)
        d = 2 * i

        @pl.when(safe)
        def _():
            shift = jnp.sqrt(qn_row * kn) + 1.0

            def tile_pair(t, l):
                kb = k_ref[0, 0, pl.ds(pl.multiple_of(t * tq, tq), tq), :]
                p = jnp.exp2(_dot(kb, qt[...]) - shift)
                vt = jnp.concatenate([vt_ref[0, 0, 2 * t], vt_ref[0, 0, 2 * t + 1]], axis=1)
                acc_ref[...] += _dot(vt, p.astype(BF16))
                return l + jnp.sum(p, axis=0, keepdims=True)

            first = first_ref[i]
            l_ref[0:1, :] = jnp.zeros((1, tq), F32)

            @pl.when((first & 1) == 1)
            def _():
                p = jnp.exp2(_dot(kv_rows(first), qt[...]) - shift)
                acc_ref[...] += _dot(vt_ref[0, 0, first], p.astype(BF16))
                l_ref[0:1, :] = jnp.sum(p, axis=0, keepdims=True)

            l = lax.fori_loop(lax.shift_right_logical(first + 1, 1), i, tile_pair, l_ref[0:1, :])
            s_lo = _dot(kv_rows(d), qt[...])
            s_hi = _dot(kv_rows(d + 1), qt[:, half:])
            p = jnp.exp2(causal(s_lo) - shift)
            p_hi = jnp.exp2(causal(s_hi) - shift[:, half:])
            acc_ref[...] += _dot(vt_ref[0, 0, d], p.astype(BF16))
            acc_ref[:, half:] += _dot(vt_ref[0, 0, d + 1], p_hi.astype(BF16))
            l = l + jnp.sum(p, axis=0, keepdims=True)
            l_ref[0:1, :] = jnp.concatenate(
                [l[:, :half], l[:, half:] + jnp.sum(p_hi, axis=0, keepdims=True)], axis=1)

        @pl.when(jnp.logical_not(safe))
        def _():
            def online(s, j, carry, cols):
                m, l = carry
                m_new = jnp.maximum(m, jnp.max(s, axis=0, keepdims=True))
                alpha = jnp.exp2(m - m_new)
                p = jnp.exp2(s - m_new)
                acc_ref[:, cols] = alpha * acc_ref[:, cols] + _dot(vt_ref[0, 0, j], p.astype(BF16))
                return m_new, alpha * l + jnp.sum(p, axis=0, keepdims=True)

            every = slice(0, tq)
            init = (jnp.full((1, tq), MASK_VALUE, F32), jnp.zeros((1, tq), F32))
            carry = lax.fori_loop(
                0, d, lambda j, c: online(_dot(kv_rows(j), qt[...]), j, c, every), init)
            m, l = online(causal(_dot(kv_rows(d), qt[...])), d, carry, every)
            upper = slice(half, tq)
            _, l_hi = online(causal(_dot(kv_rows(d + 1), qt[:, half:])), d + 1,
                             (m[:, half:], l[:, half:]), upper)
            l_ref[0:1, :] = jnp.concatenate([l[:, :half], l_hi], axis=1)

        o = (acc_ref[...] / l_ref[0:1, :]).T
        rows = pl.ds(pl.multiple_of(i * tq, tq), tq)
        o_ref[rows, :] = (ga_ref[rows, :].astype(F32) * o + gob_ref[rows, :].astype(F32)).astype(BF16)
        return 0

    lax.fori_loop(0, k_ref.shape[2] // tq, query_tile, 0)


def _attn(qt, k, vt, ga, gob):
    batch, n_heads, seq, _ = k.shape
    tq = Q_TILE
    head_cols = lambda b, h: (b, h)
    whole = lambda b, h: (b, h, 0, 0, 0)
    return pl.pallas_call(
        _attn_kernel,
        grid=(batch, n_heads),
        scratch_shapes=[pltpu.VMEM((HEAD_DIM, tq), F32), pltpu.VMEM((8, tq), F32),
                        pltpu.SMEM((seq // tq,), jnp.int32)],
        in_specs=[
            pl.BlockSpec((1, 1, seq // tq, QK_WIDTH, tq), whole),
            pl.BlockSpec((1, 1, seq, QK_WIDTH), lambda b, h: (b, h, 0, 0)),
            pl.BlockSpec((1, 1, seq // KV_TILE, HEAD_DIM, KV_TILE), whole),
            pl.BlockSpec((seq, HEAD_DIM), head_cols),
            pl.BlockSpec((seq, HEAD_DIM), head_cols),
        ],
        out_specs=pl.BlockSpec((seq, HEAD_DIM), head_cols),
        out_shape=jax.ShapeDtypeStruct(ga.shape, BF16),
        compiler_params=pltpu.CompilerParams(
            dimension_semantics=("parallel", "parallel"), vmem_limit_bytes=VMEM_LIMIT_BYTES),
        name="attn",
    )(qt, k, vt, ga, gob)


def _pack_w_in(w_in):
    d = w_in.shape[0]
    col_f = 3 * D_MODEL
    first_col = {SEC_Q: 0, SEC_K: D_MODEL, SEC_V: 2 * D_MODEL, SEC_U: col_f + N_HEADS,
                 SEC_SV: col_f + N_HEADS + D_MODEL, SEC_GA: col_f + N_HEADS + 2 * D_MODEL,
                 SEC_GB: col_f + N_HEADS + 3 * D_MODEL}
    starts = [first_col[sec] for sec in range(N_SECTIONS)]
    n_c = N_HEADS // HEADS_PER_STEP
    secs = [w_in[:, s:s + D_MODEL].reshape(d, n_c, 1, HEADS_PER_STEP * HEAD_DIM) for s in starts]
    wmain = jnp.concatenate(secs, axis=2).reshape(d, n_c * CHUNK_COLS).astype(BF16)
    wf = w_in[:, col_f:col_f + N_HEADS]
    return wmain, _replicate_forget(wf).astype(BF16)


def _replicate_forget(a):
    rows = a.shape[0]
    rep = jnp.tile(a, (1, C_PARTS))
    gap = jnp.zeros((rows, K_PART_BASE - C_PARTS * N_HEADS), a.dtype)
    tail = jnp.zeros((rows, LANES - K_PART_BASE - C_PARTS * N_HEADS), a.dtype)
    return jnp.concatenate([rep, gap, rep, tail], axis=1)


def kernel(x, ffn1_pre_g, ffn1_w_gate, ffn1_w_up, ffn1_w_down, ffn1_post_g, mix_pre_g, w_in, b_forget, sgu_ln_g, sgu_ln_b, sgu_w_s, sgu_b_s, w_out, mix_post_g, ffn2_pre_g, ffn2_w_gate, ffn2_w_up, ffn2_w_down, ffn2_post_g):
    batch, seq, d = x.shape
    depth = ffn1_pre_g.shape[0]
    tm_ffn = min(FFN_ROWS, seq)
    tm_proj = min(PROJ_ROWS, seq)
    xf = x.reshape(batch * seq, d)
    for l in range(depth):
        row = lambda a: a[l].reshape(1, -1)
        xf, h = _ffn(xf, row(ffn1_pre_g), ffn1_w_gate[l].astype(BF16), ffn1_w_up[l].astype(BF16),
                     ffn1_w_down[l].astype(BF16), row(ffn1_post_g), next_g=row(mix_pre_g),
                     tm=min(FFN_ROWS_PLAIN, seq), tf=FFN_CHUNK)
        wmain, wf = _pack_w_in(w_in[l])
        qt, k, vt, ga, gob = _proj(
            h, wmain, wf, _replicate_forget(row(b_forget)),
            row(sgu_ln_g), row(sgu_ln_b), sgu_w_s[l], sgu_b_s[l].reshape(N_HEADS, SGU_LEN, 1),
            batch=batch, seq=seq, tm=tm_proj)
        merged = _attn(qt, k, vt, ga, gob)
        xf = _ffn(xf, row(ffn2_pre_g), ffn2_w_gate[l].astype(BF16), ffn2_w_up[l].astype(BF16),
                  ffn2_w_down[l].astype(BF16), row(ffn2_post_g),
                  mix=(merged, w_out[l].astype(BF16), row(mix_post_g)), tm=tm_ffn, tf=FFN_CHUNK)
    return xf.reshape(batch, seq, d)
```

```python
import functools
import math

import jax
import jax.numpy as jnp
from jax import lax
from jax.experimental import pallas as pl
from jax.experimental.pallas import tpu as pltpu

D_MODEL = 1024
D_FF = 4 * D_MODEL
HEAD_DIM = 128
N_HEADS = D_MODEL // HEAD_DIM
GROUP_DIM = 128
SGU_LEN = 128
CHUNK = 64
RMS_EPS = 1e-6
LN_EPS = 1e-5

LANES = 128
BF16_ROWS = 16
HEADS_PER_STEP = 2
N_SECTIONS = 7
SEC_Q, SEC_K, SEC_V, SEC_U, SEC_SV, SEC_GA, SEC_GB = range(N_SECTIONS)
CHUNK_COLS = N_SECTIONS * HEADS_PER_STEP * HEAD_DIM
QK_WIDTH = 2 * HEAD_DIM
C_PARTS = 3
Q_PART_BASE = 0
K_PART_BASE = 32
FFN_ROWS = 1024
FFN_CHUNK = 1024
FFN_ROWS_PLAIN = 512
FFN_GROUP_ROWS = 256
PROJ_ROWS = 1024
CUMSUM_ROWS = 256
KV_TILE = 512
Q_TILE = 2 * KV_TILE
Q_NORM_LANE = 64
K_NORM_LANE = 65
KQ_NORM_LANE = 66
NORM_INFLATE = 1.02
MASK_VALUE = -1e30
LOG2E = math.log2(math.e)
SAFE_EXP2_RANGE = 64.0
SKIP_DECAY_LOG2 = 160.0
VMEM_LIMIT_BYTES = 56 * 1024 * 1024

F32 = jnp.float32
BF16 = jnp.bfloat16


def _rms(x, g):
    return x * lax.rsqrt(jnp.mean(x * x, axis=-1, keepdims=True) + RMS_EPS) * g


def _dot(a, b):
    return jnp.dot(a, b, preferred_element_type=F32)


def _gelu(x):
    return 0.5 * x * (1.0 + lax.erf(x * math.sqrt(0.5)))


def _ffn_kernel(*refs, tf, has_mix, has_next):
    refs = list(refs)
    if has_mix:
        m_ref, wo_ref, mpost_ref = refs[:3]
        refs = refs[3:]
    x_ref, pre_ref, wg_ref, wu_ref, wd_ref, post_ref = refs[:6]
    refs = refs[6:]
    if has_next:
        next_g_ref, o_ref, hn_ref = refs
    else:
        (o_ref,) = refs
    tm = x_ref.shape[0]
    groups = max(tm // FFN_GROUP_ROWS, 1)
    x_all = x_ref[...]
    if has_mix:
        x_all = x_all + _rms(_dot(m_ref[...], wo_ref[...]), mpost_ref[...])
    for r in range(groups):
        rows = slice(r * tm // groups, (r + 1) * tm // groups)
        x = x_all[rows, :]
        h = _rms(x, pre_ref[...]).astype(BF16)
        y = None
        for f in range(D_FF // tf):
            cols = slice(f * tf, (f + 1) * tf)
            g = _dot(h, wg_ref[:, cols])
            u = _dot(h, wu_ref[:, cols])
            a = (g * jax.nn.sigmoid(g) * u).astype(BF16)
            part = _dot(a, wd_ref[cols, :])
            y = part if y is None else y + part
        out = x + 0.5 * _rms(y, post_ref[...])
        o_ref[rows, :] = out
        if has_next:
            hn_ref[rows, :] = _rms(out, next_g_ref[...]).astype(BF16)


def _ffn(x, pre_g, wg, wu, wd, post_g, mix=None, next_g=None, *, tm, tf):
    n, d = x.shape
    row = lambda i: (i, 0)
    resident = lambda shape: pl.BlockSpec(shape, lambda i: (0, 0), pipeline_mode=pl.Buffered(1))
    in_specs = [
        pl.BlockSpec((tm, d), row),
        resident((1, d)),
        resident((d, D_FF)),
        resident((d, D_FF)),
        resident((D_FF, d)),
        resident((1, d)),
    ]
    args = [x, pre_g, wg, wu, wd, post_g]
    if mix is not None:
        merged, w_out, mix_post_g = mix
        in_specs = [pl.BlockSpec((tm, d), row), resident((d, d)), resident((1, d))] + in_specs
        args = [merged, w_out, mix_post_g] + args
    out_specs = pl.BlockSpec((tm, d), row)
    out_shape = jax.ShapeDtypeStruct((n, d), F32)
    if next_g is not None:
        in_specs.append(resident((1, d)))
        args.append(next_g)
        out_specs = [out_specs, pl.BlockSpec((tm, d), row)]
        out_shape = [out_shape, jax.ShapeDtypeStruct((n, d), BF16)]
    return pl.pallas_call(
        functools.partial(_ffn_kernel, tf=tf, has_mix=mix is not None,
                          has_next=next_g is not None),
        grid=(n // tm,),
        in_specs=in_specs,
        out_specs=out_specs,
        out_shape=out_shape,
        compiler_params=pltpu.CompilerParams(
            dimension_semantics=("parallel",), vmem_limit_bytes=VMEM_LIMIT_BYTES),
        name="ffn_mix" if mix is not None else "ffn",
    )(*args)


def _cumsum_rows(lf, carry):
    rows = lf.shape[0]
    r = lax.broadcasted_iota(jnp.int32, (CUMSUM_ROWS, CUMSUM_ROWS), 0)
    c = lax.broadcasted_iota(jnp.int32, (CUMSUM_ROWS, CUMSUM_ROWS), 1)
    tri = (c <= r).astype(BF16)
    t1 = lf.astype(BF16)
    r1 = lf - t1.astype(F32)
    t2 = r1.astype(BF16)
    t3 = (r1 - t2.astype(F32)).astype(BF16)
    n_blk = rows // CUMSUM_ROWS
    wide = jnp.concatenate(
        [t[b * CUMSUM_ROWS:(b + 1) * CUMSUM_ROWS] for t in (t1, t2, t3) for b in range(n_blk)],
        axis=1)
    local = _dot(tri, wide)
    out = []
    for b in range(n_blk):
        parts = [local[:, (p * n_blk + b) * LANES:(p * n_blk + b + 1) * LANES] for p in range(3)]
        cs = (parts[0] + parts[1]) + parts[2] + carry
        carry = cs[CUMSUM_ROWS - 1:CUMSUM_ROWS, :]
        out.append(cs)
    return jnp.concatenate(out, axis=0), carry


def _proj_kernel(h_ref, wmain_ref, wf_ref, bf_ref, lng_ref, lnb_ref, ws_ref, bs_ref,
                 qt_ref, k_ref, vt_ref, ga_ref, gob_ref,
                 cpart_ref, carry_ref, *, tm, blocks_per_seq):
    i = pl.program_id(0)
    c = pl.program_id(1)
    lane = lax.broadcasted_iota(jnp.int32, (tm, LANES), 1)
    in_q_parts = lane < Q_PART_BASE + C_PARTS * N_HEADS
    in_k_parts = (lane >= K_PART_BASE) & (lane < K_PART_BASE + C_PARTS * N_HEADS)

    @pl.when(c == 0)
    def _():
        @pl.when(i % blocks_per_seq == 0)
        def _():
            carry_ref[...] = jnp.zeros_like(carry_ref)

        lf = jax.nn.log_sigmoid(_dot(h_ref[...], wf_ref[...]) + bf_ref[...])
        lf = jnp.where(in_q_parts | in_k_parts, lf, 0.0)
        cs, carry = _cumsum_rows(lf, carry_ref[0:1, :])
        carry_ref[0:1, :] = carry
        cs = cs * LOG2E
        hi = cs.astype(BF16).astype(F32)
        r1 = cs - hi
        mid = r1.astype(BF16).astype(F32)
        lo = r1 - mid
        part = (lane % 32) // N_HEADS
        cpart_ref[...] = jnp.where(part == 0, hi, jnp.where(part == 1, mid, lo))

    z = _dot(h_ref[...], wmain_ref[...])
    sec = HEADS_PER_STEP * HEAD_DIM

    def section(s, j):
        return z[:, s * sec + j * HEAD_DIM: s * sec + (j + 1) * HEAD_DIM]

    cpart = cpart_ref[...]
    scale = LOG2E / math.sqrt(HEAD_DIM)
    wr = lax.broadcasted_iota(jnp.int32, (SGU_LEN, SGU_LEN), 0)
    wc = lax.broadcasted_iota(jnp.int32, (SGU_LEN, SGU_LEN), 1)
    chunk_causal = (wc // CHUNK) <= (wr // CHUNK)
    for j in range(HEADS_PER_STEP):
        head = c * HEADS_PER_STEP + j
        is_head = (lane % N_HEADS) == head
        q = section(SEC_Q, j) * scale
        k = section(SEC_K, j)
        qn = jnp.sum(q * q, axis=-1, keepdims=True) * NORM_INFLATE
        kn = jnp.sum(k * k, axis=-1, keepdims=True) * NORM_INFLATE
        q_aug = jnp.where(in_q_parts, cpart, jnp.where(in_k_parts & is_head, 1.0, 0.0))
        q_aug = jnp.where(lane == Q_NORM_LANE, qn, q_aug)
        k_aug = jnp.where(in_k_parts, -cpart, jnp.where(in_q_parts & is_head, 1.0, 0.0))
        k_aug = jnp.where(lane == K_NORM_LANE, kn, jnp.where(lane == KQ_NORM_LANE, qn, k_aug))
        qt = jnp.concatenate([q.T, q_aug.T], axis=0).astype(BF16)
        for r in range(tm // Q_TILE):
            qt_ref[0, j, r] = qt[:, r * Q_TILE:(r + 1) * Q_TILE]
        k_ref[0, j, :, 0:HEAD_DIM] = k.astype(BF16)
        k_ref[0, j, :, HEAD_DIM:QK_WIDTH] = k_aug.astype(BF16)
        vt = section(SEC_V, j).T.astype(BF16)
        for r in range(tm // KV_TILE):
            vt_ref[0, j, r] = vt[:, r * KV_TILE:(r + 1) * KV_TILE]

        cols = slice(j * GROUP_DIM, (j + 1) * GROUP_DIM)
        u = _gelu(section(SEC_U, j))
        sv = _gelu(section(SEC_SV, j))
        mu = jnp.mean(sv, axis=-1, keepdims=True)
        dv = sv - mu
        var = jnp.mean(dv * dv, axis=-1, keepdims=True)
        vn = (dv * lax.rsqrt(var + LN_EPS) * lng_ref[:, cols] + lnb_ref[:, cols]).astype(BF16)
        w = jnp.where(chunk_causal, ws_ref[j], 0.0).astype(BF16)
        b = bs_ref[j]
        mixed = [
            _dot(w, vn[n * SGU_LEN:(n + 1) * SGU_LEN, :]) + b for n in range(tm // SGU_LEN)
        ]
        mixed = jnp.concatenate(mixed, axis=0)
        ga_ref[:, cols] = jax.nn.sigmoid(section(SEC_GA, j)).astype(BF16)
        gob_ref[:, cols] = (jax.nn.sigmoid(section(SEC_GB, j)) * u * mixed).astype(BF16)


def _proj(h, wmain, wf, bf, ln_g, ln_b, w_s, b_s, *, batch, seq, tm):
    n, d = h.shape
    n_c = N_HEADS // HEADS_PER_STEP
    bps = seq // tm
    hp = HEADS_PER_STEP
    return pl.pallas_call(
        functools.partial(_proj_kernel, tm=tm, blocks_per_seq=bps),
        grid=(n // tm, n_c),
        in_specs=[
            pl.BlockSpec((tm, d), lambda i, c: (i, 0)),
            pl.BlockSpec((d, CHUNK_COLS), lambda i, c: (0, c)),
            pl.BlockSpec((d, LANES), lambda i, c: (0, 0)),
            pl.BlockSpec((1, LANES), lambda i, c: (0, 0)),
            pl.BlockSpec((1, hp * GROUP_DIM), lambda i, c: (0, c)),
            pl.BlockSpec((1, hp * GROUP_DIM), lambda i, c: (0, c)),
            pl.BlockSpec((hp, SGU_LEN, SGU_LEN), lambda i, c: (c, 0, 0)),
            pl.BlockSpec((hp, SGU_LEN, 1), lambda i, c: (c, 0, 0)),
        ],
        out_specs=[
            pl.BlockSpec((1, hp, tm // Q_TILE, QK_WIDTH, Q_TILE),
                         lambda i, c: (i // bps, c, i % bps, 0, 0)),
            pl.BlockSpec((1, hp, tm, QK_WIDTH), lambda i, c: (i // bps, c, i % bps, 0)),
            pl.BlockSpec((1, hp, tm // KV_TILE, HEAD_DIM, KV_TILE),
                         lambda i, c: (i // bps, c, i % bps, 0, 0)),
            pl.BlockSpec((tm, hp * GROUP_DIM), lambda i, c: (i, c)),
            pl.BlockSpec((tm, hp * GROUP_DIM), lambda i, c: (i, c)),
        ],
        out_shape=[
            jax.ShapeDtypeStruct((batch, N_HEADS, seq // Q_TILE, QK_WIDTH, Q_TILE), BF16),
            jax.ShapeDtypeStruct((batch, N_HEADS, seq, QK_WIDTH), BF16),
            jax.ShapeDtypeStruct((batch, N_HEADS, seq // KV_TILE, HEAD_DIM, KV_TILE), BF16),
            jax.ShapeDtypeStruct((n, d), BF16),
            jax.ShapeDtypeStruct((n, d), BF16),
        ],
        scratch_shapes=[pltpu.VMEM((tm, LANES), F32), pltpu.VMEM((8, LANES), F32)],
        compiler_params=pltpu.CompilerParams(
            dimension_semantics=("arbitrary", "arbitrary"), vmem_limit_bytes=VMEM_LIMIT_BYTES),
        name="proj",
    )(h, wmain, wf, bf, ln_g, ln_b, w_s, b_s)


def _attn_kernel(qt_ref, k_ref, vt_ref, ga_ref, gob_ref, o_ref, acc_ref, l_ref, first_ref):
    tk, tq, half = KV_TILE, Q_TILE, Q_TILE // 2
    n_kv = k_ref.shape[2] // tk

    def kv_rows(j):
        return k_ref[0, 0, pl.ds(pl.multiple_of(j * tk, tk), tk), :]

    def causal(s):
        kv_pos = lax.broadcasted_iota(jnp.int32, s.shape, 0)
        q_pos = lax.broadcasted_iota(jnp.int32, s.shape, 1)
        return jnp.where(kv_pos <= q_pos, s, MASK_VALUE)

    def tile_max(j, best):
        rows = kv_rows(j)[:, HEAD_DIM:QK_WIDTH].reshape(tk // BF16_ROWS, BF16_ROWS, LANES)
        return jnp.maximum(best, jnp.max(rows, axis=0))
    best = lax.fori_loop(0, n_kv, tile_max, jnp.zeros((BF16_ROWS, LANES), BF16))
    best = jnp.max(best.astype(F32), axis=0, keepdims=True)
    lane = lax.broadcasted_iota(jnp.int32, (1, LANES), 1)
    kn = jnp.max(jnp.where(lane == K_NORM_LANE, best, 0.0), axis=1, keepdims=True)
    qn = jnp.max(jnp.where(lane == KQ_NORM_LANE, best, 0.0), axis=1, keepdims=True)
    safe = 2.0 * jnp.sqrt(jnp.max(kn * qn)) <= SAFE_EXP2_RANGE

    head = pl.program_id(1)
    mine = (lane >= K_PART_BASE) & (lane < K_PART_BASE + C_PARTS * N_HEADS) & (lane % N_HEADS == head)

    def neg_c(row):
        r = k_ref[0, 0, row:row + 1, HEAD_DIM:QK_WIDTH].astype(F32)
        return jnp.sum(jnp.where(mine, r, 0.0), axis=1, keepdims=True)

    n_q = k_ref.shape[2] // tq
    tile_end = [neg_c((j + 1) * tk - 1) for j in range(2 * (n_q - 1))]
    for i in range(n_q):
        q_start = neg_c(i * tq)
        dead = [(q_start - tile_end[j] > SKIP_DECAY_LOG2).astype(jnp.int32) for j in range(2 * i)]
        first_ref[i] = sum(dead)[0, 0] if dead else 0

    def query_tile(i, _):
        qt = qt_ref.at[0, 0, i]
        qn_row = qt[HEAD_DIM + Q_NORM_LANE:HEAD_DIM + Q_NORM_LANE + 1, :].astype(F32)
        acc_ref[...] = jnp.zeros_like(acc_ref)
        d = 2 * i

        @pl.when(safe)
        def _():
            shift = jnp.sqrt(qn_row * kn) + 1.0

            def tiles(j, n, l):
                kb = k_ref[0, 0, pl.ds(pl.multiple_of(j * tk, tk), n * tk), :]
                p = jnp.exp2(_dot(kb, qt[...]) - shift)
                vt = jnp.concatenate([vt_ref[0, 0, j + r] for r in range(n)], axis=1)
                acc_ref[...] += _dot(vt, p.astype(BF16))
                return l + jnp.sum(p, axis=0, keepdims=True)

            first = first_ref[i]
            l_ref[0:1, :] = jnp.zeros((1, tq), F32)

            @pl.when((first & 1) == 1)
            def _():
                l_ref[0:1, :] = tiles(first, 1, l_ref[0:1, :])

            pair0 = first + (first & 1)
            odd_pairs = lax.shift_right_logical(d - pair0, 1) & 1

            @pl.when(odd_pairs == 1)
            def _():
                l_ref[0:1, :] = tiles(pair0, 2, l_ref[0:1, :])

            quad0 = pair0 + 2 * odd_pairs
            l = lax.fori_loop(0, lax.shift_right_logical(d - quad0, 2),
                              lambda t, l: tiles(quad0 + 4 * t, 4, l), l_ref[0:1, :])
            s_lo = _dot(kv_rows(d), qt[...])
            s_hi = _dot(kv_rows(d + 1), qt[:, half:])
            p = jnp.exp2(causal(s_lo) - shift)
            p_hi = jnp.exp2(causal(s_hi) - shift[:, half:])
            acc_ref[...] += _dot(vt_ref[0, 0, d], p.astype(BF16))
            acc_ref[:, half:] += _dot(vt_ref[0, 0, d + 1], p_hi.astype(BF16))
            l = l + jnp.sum(p, axis=0, keepdims=True)
            l_ref[0:1, :] = jnp.concatenate(
                [l[:, :half], l[:, half:] + jnp.sum(p_hi, axis=0, keepdims=True)], axis=1)

        @pl.when(jnp.logical_not(safe))
        def _():
            def online(s, j, carry, cols):
                m, l = carry
                m_new = jnp.maximum(m, jnp.max(s, axis=0, keepdims=True))
                alpha = jnp.exp2(m - m_new)
                p = jnp.exp2(s - m_new)
                acc_ref[:, cols] = alpha * acc_ref[:, cols] + _dot(vt_ref[0, 0, j], p.astype(BF16))
                return m_new, alpha * l + jnp.sum(p, axis=0, keepdims=True)

            every = slice(0, tq)
            init = (jnp.full((1, tq), MASK_VALUE, F32), jnp.zeros((1, tq), F32))
            carry = lax.fori_loop(
                0, d, lambda j, c: online(_dot(kv_rows(j), qt[...]), j, c, every), init)
            m, l = online(causal(_dot(kv_rows(d), qt[...])), d, carry, every)
            upper = slice(half, tq)
            _, l_hi = online(causal(_dot(kv_rows(d + 1), qt[:, half:])), d + 1,
                             (m[:, half:], l[:, half:]), upper)
            l_ref[0:1, :] = jnp.concatenate([l[:, :half], l_hi], axis=1)

        o = (acc_ref[...] / l_ref[0:1, :]).T
        rows = pl.ds(pl.multiple_of(i * tq, tq), tq)
        o_ref[rows, :] = (ga_ref[rows, :].astype(F32) * o + gob_ref[rows, :].astype(F32)).astype(BF16)
        return 0

    lax.fori_loop(0, k_ref.shape[2] // tq, query_tile, 0)


def _attn(qt, k, vt, ga, gob):
    batch, n_heads, seq, _ = k.shape
    tq = Q_TILE
    head_cols = lambda b, h: (b, h)
    whole = lambda b, h: (b, h, 0, 0, 0)
    return pl.pallas_call(
        _attn_kernel,
        grid=(batch, n_heads),
        scratch_shapes=[pltpu.VMEM((HEAD_DIM, tq), F32), pltpu.VMEM((8, tq), F32),
                        pltpu.SMEM((seq // tq,), jnp.int32)],
        in_specs=[
            pl.BlockSpec((1, 1, seq // tq, QK_WIDTH, tq), whole),
            pl.BlockSpec((1, 1, seq, QK_WIDTH), lambda b, h: (b, h, 0, 0)),
            pl.BlockSpec((1, 1, seq // KV_TILE, HEAD_DIM, KV_TILE), whole),
            pl.BlockSpec((seq, HEAD_DIM), head_cols),
            pl.BlockSpec((seq, HEAD_DIM), head_cols),
        ],
        out_specs=pl.BlockSpec((seq, HEAD_DIM), head_cols),
        out_shape=jax.ShapeDtypeStruct(ga.shape, BF16),
        compiler_params=pltpu.CompilerParams(
            dimension_semantics=("parallel", "parallel"), vmem_limit_bytes=VMEM_LIMIT_BYTES),
        name="attn",
    )(qt, k, vt, ga, gob)


def _pack_w_in(w_in):
    d = w_in.shape[0]
    col_f = 3 * D_MODEL
    first_col = {SEC_Q: 0, SEC_K: D_MODEL, SEC_V: 2 * D_MODEL, SEC_U: col_f + N_HEADS,
                 SEC_SV: col_f + N_HEADS + D_MODEL, SEC_GA: col_f + N_HEADS + 2 * D_MODEL,
                 SEC_GB: col_f + N_HEADS + 3 * D_MODEL}
    starts = [first_col[sec] for sec in range(N_SECTIONS)]
    n_c = N_HEADS // HEADS_PER_STEP
    secs = [w_in[:, s:s + D_MODEL].reshape(d, n_c, 1, HEADS_PER_STEP * HEAD_DIM) for s in starts]
    wmain = jnp.concatenate(secs, axis=2).reshape(d, n_c * CHUNK_COLS).astype(BF16)
    wf = w_in[:, col_f:col_f + N_HEADS]
    return wmain, _replicate_forget(wf).astype(BF16)


def _replicate_forget(a):
    rows = a.shape[0]
    rep = jnp.tile(a, (1, C_PARTS))
    gap = jnp.zeros((rows, K_PART_BASE - C_PARTS * N_HEADS), a.dtype)
    tail = jnp.zeros((rows, LANES - K_PART_BASE - C_PARTS * N_HEADS), a.dtype)
    return jnp.concatenate([rep, gap, rep, tail], axis=1)


def kernel(x, ffn1_pre_g, ffn1_w_gate, ffn1_w_up, ffn1_w_down, ffn1_post_g, mix_pre_g, w_in, b_forget, sgu_ln_g, sgu_ln_b, sgu_w_s, sgu_b_s, w_out, mix_post_g, ffn2_pre_g, ffn2_w_gate, ffn2_w_up, ffn2_w_down, ffn2_post_g):
    batch, seq, d = x.shape
    depth = ffn1_pre_g.shape[0]
    tm_ffn = min(FFN_ROWS, seq)
    tm_proj = min(PROJ_ROWS, seq)
    xf = x.reshape(batch * seq, d)
    for l in range(depth):
        row = lambda a: a[l].reshape(1, -1)
        xf, h = _ffn(xf, row(ffn1_pre_g), ffn1_w_gate[l].astype(BF16), ffn1_w_up[l].astype(BF16),
                     ffn1_w_down[l].astype(BF16), row(ffn1_post_g), next_g=row(mix_pre_g),
                     tm=min(FFN_ROWS_PLAIN, seq), tf=FFN_CHUNK)
        wmain, wf = _pack_w_in(w_in[l])
        qt, k, vt, ga, gob = _proj(
            h, wmain, wf, _replicate_forget(row(b_forget)),
            row(sgu_ln_g), row(sgu_ln_b), sgu_w_s[l], sgu_b_s[l].reshape(N_HEADS, SGU_LEN, 1),
            batch=batch, seq=seq, tm=tm_proj)
        merged = _attn(qt, k, vt, ga, gob)
        xf = _ffn(xf, row(ffn2_pre_g), ffn2_w_gate[l].astype(BF16), ffn2_w_up[l].astype(BF16),
                  ffn2_w_down[l].astype(BF16), row(ffn2_post_g),
                  mix=(merged, w_out[l].astype(BF16), row(mix_post_g)), tm=tm_ffn, tf=FFN_CHUNK)
    return xf.reshape(batch, seq, d)
```

```python
import functools
import math

import jax
import jax.numpy as jnp
from jax import lax
from jax.experimental import pallas as pl
from jax.experimental.pallas import tpu as pltpu

D_MODEL = 1024
D_FF = 4 * D_MODEL
HEAD_DIM = 128
N_HEADS = D_MODEL // HEAD_DIM
GROUP_DIM = 128
SGU_LEN = 128
CHUNK = 64
RMS_EPS = 1e-6
LN_EPS = 1e-5

LANES = 128
BF16_ROWS = 16
HEADS_PER_STEP = 2
N_SECTIONS = 7
SEC_Q, SEC_K, SEC_V, SEC_U, SEC_SV, SEC_GA, SEC_GB = range(N_SECTIONS)
CHUNK_COLS = N_SECTIONS * HEADS_PER_STEP * HEAD_DIM
QK_WIDTH = 2 * HEAD_DIM
C_PARTS = 3
Q_PART_BASE = 0
K_PART_BASE = 32
FFN_ROWS = 1024
FFN_CHUNK = 1024
FFN_ROWS_PLAIN = 512
FFN_GROUP_ROWS = 256
PROJ_ROWS = 1024
CUMSUM_ROWS = 256
KV_TILE = 512
Q_TILE = 2 * KV_TILE
Q_NORM_LANE = 64
K_NORM_LANE = 65
KQ_NORM_LANE = 66
NORM_INFLATE = 1.02
MASK_VALUE = -1e30
LOG2E = math.log2(math.e)
SAFE_EXP2_RANGE = 64.0
SKIP_DECAY_LOG2 = 160.0
VMEM_LIMIT_BYTES = 56 * 1024 * 1024

F32 = jnp.float32
BF16 = jnp.bfloat16


def _rms(x, g):
    return x * lax.rsqrt(jnp.mean(x * x, axis=-1, keepdims=True) + RMS_EPS) * g


def _dot(a, b):
    return jnp.dot(a, b, preferred_element_type=F32)


def _gelu(x):
    return 0.5 * x * (1.0 + lax.erf(x * math.sqrt(0.5)))


def _ffn_kernel(*refs, tf, has_mix, has_next):
    refs = list(refs)
    if has_mix:
        m_ref, wo_ref, mpost_ref = refs[:3]
        refs = refs[3:]
    x_ref, pre_ref, wg_ref, wu_ref, wd_ref, post_ref = refs[:6]
    refs = refs[6:]
    if has_next:
        next_g_ref, o_ref, hn_ref = refs
    else:
        (o_ref,) = refs
    tm = x_ref.shape[0]
    groups = max(tm // FFN_GROUP_ROWS, 1)
    x_all = x_ref[...]
    if has_mix:
        x_all = x_all + _rms(_dot(m_ref[...], wo_ref[...]), mpost_ref[...])
    for r in range(groups):
        rows = slice(r * tm // groups, (r + 1) * tm // groups)
        x = x_all[rows, :]
        h = _rms(x, pre_ref[...]).astype(BF16)
        y = None
        for f in range(D_FF // tf):
            cols = slice(f * tf, (f + 1) * tf)
            g = _dot(h, wg_ref[:, cols])
            u = _dot(h, wu_ref[:, cols])
            a = (g * jax.nn.sigmoid(g) * u).astype(BF16)
            part = _dot(a, wd_ref[cols, :])
            y = part if y is None else y + part
        out = x + 0.5 * _rms(y, post_ref[...])
        o_ref[rows, :] = out
        if has_next:
            hn_ref[rows, :] = _rms(out, next_g_ref[...]).astype(BF16)


def _ffn(x, pre_g, wg, wu, wd, post_g, mix=None, next_g=None, *, tm, tf):
    n, d = x.shape
    row = lambda i: (i, 0)
    resident = lambda shape: pl.BlockSpec(shape, lambda i: (0, 0), pipeline_mode=pl.Buffered(1))
    in_specs = [
        pl.BlockSpec((tm, d), row),
        resident((1, d)),
        resident((d, D_FF)),
        resident((d, D_FF)),
        resident((D_FF, d)),
        resident((1, d)),
    ]
    args = [x, pre_g, wg, wu, wd, post_g]
    if mix is not None:
        merged, w_out, mix_post_g = mix
        in_specs = [pl.BlockSpec((tm, d), row), resident((d, d)), resident((1, d))] + in_specs
        args = [merged, w_out, mix_post_g] + args
    out_specs = pl.BlockSpec((tm, d), row)
    out_shape = jax.ShapeDtypeStruct((n, d), F32)
    if next_g is not None:
        in_specs.append(resident((1, d)))
        args.append(next_g)
        out_specs = [out_specs, pl.BlockSpec((tm, d), row)]
        out_shape = [out_shape, jax.ShapeDtypeStruct((n, d), BF16)]
    return pl.pallas_call(
        functools.partial(_ffn_kernel, tf=tf, has_mix=mix is not None,
                          has_next=next_g is not None),
        grid=(n // tm,),
        in_specs=in_specs,
        out_specs=out_specs,
        out_shape=out_shape,
        compiler_params=pltpu.CompilerParams(
            dimension_semantics=("parallel",), vmem_limit_bytes=VMEM_LIMIT_BYTES),
        name="ffn_mix" if mix is not None else "ffn",
    )(*args)


def _cumsum_rows(lf, carry):
    rows = lf.shape[0]
    r = lax.broadcasted_iota(jnp.int32, (CUMSUM_ROWS, CUMSUM_ROWS), 0)
    c = lax.broadcasted_iota(jnp.int32, (CUMSUM_ROWS, CUMSUM_ROWS), 1)
    tri = (c <= r).astype(BF16)
    t1 = lf.astype(BF16)
    r1 = lf - t1.astype(F32)
    t2 = r1.astype(BF16)
    t3 = (r1 - t2.astype(F32)).astype(BF16)
    n_blk = rows // CUMSUM_ROWS
    wide = jnp.concatenate(
        [t[b * CUMSUM_ROWS:(b + 1) * CUMSUM_ROWS] for t in (t1, t2, t3) for b in range(n_blk)],
        axis=1)
    local = _dot(tri, wide)
    out = []
    for b in range(n_blk):
        parts = [local[:, (p * n_blk + b) * LANES:(p * n_blk + b + 1) * LANES] for p in range(3)]
        cs = (parts[0] + parts[1]) + parts[2] + carry
        carry = cs[CUMSUM_ROWS - 1:CUMSUM_ROWS, :]
        out.append(cs)
    return jnp.concatenate(out, axis=0), carry


def _proj_kernel(h_ref, wmain_ref, wf_ref, bf_ref, lng_ref, lnb_ref, ws_ref, bs_ref,
                 qt_ref, k_ref, vt_ref, ga_ref, gob_ref,
                 cpart_ref, carry_ref, *, tm, blocks_per_seq):
    i = pl.program_id(0)
    c = pl.program_id(1)
    lane = lax.broadcasted_iota(jnp.int32, (tm, LANES), 1)
    in_q_parts = lane < Q_PART_BASE + C_PARTS * N_HEADS
    in_k_parts = (lane >= K_PART_BASE) & (lane < K_PART_BASE + C_PARTS * N_HEADS)

    @pl.when(c == 0)
    def _():
        @pl.when(i % blocks_per_seq == 0)
        def _():
            carry_ref[...] = jnp.zeros_like(carry_ref)

        lf = jax.nn.log_sigmoid(_dot(h_ref[...], wf_ref[...]) + bf_ref[...])
        lf = jnp.where(in_q_parts | in_k_parts, lf, 0.0)
        cs, carry = _cumsum_rows(lf, carry_ref[0:1, :])
        carry_ref[0:1, :] = carry
        cs = cs * LOG2E
        hi = cs.astype(BF16).astype(F32)
        r1 = cs - hi
        mid = r1.astype(BF16).astype(F32)
        lo = r1 - mid
        part = (lane % 32) // N_HEADS
        cpart_ref[...] = jnp.where(part == 0, hi, jnp.where(part == 1, mid, lo))

    z = _dot(h_ref[...], wmain_ref[...])
    sec = HEADS_PER_STEP * HEAD_DIM

    def section(s, j):
        return z[:, s * sec + j * HEAD_DIM: s * sec + (j + 1) * HEAD_DIM]

    cpart = cpart_ref[...]
    scale = LOG2E / math.sqrt(HEAD_DIM)
    wr = lax.broadcasted_iota(jnp.int32, (SGU_LEN, SGU_LEN), 0)
    wc = lax.broadcasted_iota(jnp.int32, (SGU_LEN, SGU_LEN), 1)
    chunk_causal = (wc // CHUNK) <= (wr // CHUNK)
    for j in range(HEADS_PER_STEP):
        head = c * HEADS_PER_STEP + j
        is_head = (lane % N_HEADS) == head
        q = section(SEC_Q, j) * scale
        k = section(SEC_K, j)
        qn = jnp.sum(q * q, axis=-1, keepdims=True) * NORM_INFLATE
        kn = jnp.sum(k * k, axis=-1, keepdims=True) * NORM_INFLATE
        q_aug = jnp.where(in_q_parts, cpart, jnp.where(in_k_parts & is_head, 1.0, 0.0))
        q_aug = jnp.where(lane == Q_NORM_LANE, qn, q_aug)
        k_aug = jnp.where(in_k_parts, -cpart, jnp.where(in_q_parts & is_head, 1.0, 0.0))
        k_aug = jnp.where(lane == K_NORM_LANE, kn, jnp.where(lane == KQ_NORM_LANE, qn, k_aug))
        qt = jnp.concatenate([q.T, q_aug.T], axis=0).astype(BF16)
        for r in range(tm // Q_TILE):
            qt_ref[0, j, r] = qt[:, r * Q_TILE:(r + 1) * Q_TILE]
        k_ref[0, j, :, 0:HEAD_DIM] = k.astype(BF16)
        k_ref[0, j, :, HEAD_DIM:QK_WIDTH] = k_aug.astype(BF16)
        vt = section(SEC_V, j).T.astype(BF16)
        for r in range(tm // KV_TILE):
            vt_ref[0, j, r] = vt[:, r * KV_TILE:(r + 1) * KV_TILE]

        cols = slice(j * GROUP_DIM, (j + 1) * GROUP_DIM)
        u = _gelu(section(SEC_U, j))
        sv = _gelu(section(SEC_SV, j))
        mu = jnp.mean(sv, axis=-1, keepdims=True)
        dv = sv - mu
        var = jnp.mean(dv * dv, axis=-1, keepdims=True)
        vn = (dv * lax.rsqrt(var + LN_EPS) * lng_ref[:, cols] + lnb_ref[:, cols]).astype(BF16)
        w = jnp.where(chunk_causal, ws_ref[j], 0.0).astype(BF16)
        b = bs_ref[j]
        mixed = [
            _dot(w, vn[n * SGU_LEN:(n + 1) * SGU_LEN, :]) + b for n in range(tm // SGU_LEN)
        ]
        mixed = jnp.concatenate(mixed, axis=0)
        ga_ref[:, cols] = jax.nn.sigmoid(section(SEC_GA, j)).astype(BF16)
        gob_ref[:, cols] = (jax.nn.sigmoid(section(SEC_GB, j)) * u * mixed).astype(BF16)


def _proj(h, wmain, wf, bf, ln_g, ln_b, w_s, b_s, *, batch, seq, tm):
    n, d = h.shape
    n_c = N_HEADS // HEADS_PER_STEP
    bps = seq // tm
    hp = HEADS_PER_STEP
    return pl.pallas_call(
        functools.partial(_proj_kernel, tm=tm, blocks_per_seq=bps),
        grid=(n // tm, n_c),
        in_specs=[
            pl.BlockSpec((tm, d), lambda i, c: (i, 0)),
            pl.BlockSpec((d, CHUNK_COLS), lambda i, c: (0, c)),
            pl.BlockSpec((d, LANES), lambda i, c: (0, 0)),
            pl.BlockSpec((1, LANES), lambda i, c: (0, 0)),
            pl.BlockSpec((1, hp * GROUP_DIM), lambda i, c: (0, c)),
            pl.BlockSpec((1, hp * GROUP_DIM), lambda i, c: (0, c)),
            pl.BlockSpec((hp, SGU_LEN, SGU_LEN), lambda i, c: (c, 0, 0)),
            pl.BlockSpec((hp, SGU_LEN, 1), lambda i, c: (c, 0, 0)),
        ],
        out_specs=[
            pl.BlockSpec((1, hp, tm // Q_TILE, QK_WIDTH, Q_TILE),
                         lambda i, c: (i // bps, c, i % bps, 0, 0)),
            pl.BlockSpec((1, hp, tm, QK_WIDTH), lambda i, c: (i // bps, c, i % bps, 0)),
            pl.BlockSpec((1, hp, tm // KV_TILE, HEAD_DIM, KV_TILE),
                         lambda i, c: (i // bps, c, i % bps, 0, 0)),
            pl.BlockSpec((tm, hp * GROUP_DIM), lambda i, c: (i, c)),
            pl.BlockSpec((tm, hp * GROUP_DIM), lambda i, c: (i, c)),
        ],
        out_shape=[
            jax.ShapeDtypeStruct((batch, N_HEADS, seq // Q_TILE, QK_WIDTH, Q_TILE), BF16),
            jax.ShapeDtypeStruct((batch, N_HEADS, seq, QK_WIDTH), BF16),
            jax.ShapeDtypeStruct((batch, N_HEADS, seq // KV_TILE, HEAD_DIM, KV_TILE), BF16),
            jax.ShapeDtypeStruct((n, d), BF16),
            jax.ShapeDtypeStruct((n, d), BF16),
        ],
        scratch_shapes=[pltpu.VMEM((tm, LANES), F32), pltpu.VMEM((8, LANES), F32)],
        compiler_params=pltpu.CompilerParams(
            dimension_semantics=("arbitrary", "arbitrary"), vmem_limit_bytes=VMEM_LIMIT_BYTES),
        name="proj",
    )(h, wmain, wf, bf, ln_g, ln_b, w_s, b_s)


def _attn_kernel(qt_ref, k_ref, vt_ref, ga_ref, gob_ref, o_ref, acc_all, l_all, first_ref):
    tk, tq, half = KV_TILE, Q_TILE, Q_TILE // 2
    n_kv = k_ref.shape[2] // tk

    def kv_rows(j):
        return k_ref[0, 0, pl.ds(pl.multiple_of(j * tk, tk), tk), :]

    def causal(s):
        kv_pos = lax.broadcasted_iota(jnp.int32, s.shape, 0)
        q_pos = lax.broadcasted_iota(jnp.int32, s.shape, 1)
        return jnp.where(kv_pos <= q_pos, s, MASK_VALUE)

    def tile_max(j, best):
        rows = kv_rows(j)[:, HEAD_DIM:QK_WIDTH].reshape(tk // BF16_ROWS, BF16_ROWS, LANES)
        return jnp.maximum(best, jnp.max(rows, axis=0))
    best = lax.fori_loop(0, n_kv, tile_max, jnp.zeros((BF16_ROWS, LANES), BF16))
    best = jnp.max(best.astype(F32), axis=0, keepdims=True)
    lane = lax.broadcasted_iota(jnp.int32, (1, LANES), 1)
    kn = jnp.max(jnp.where(lane == K_NORM_LANE, best, 0.0), axis=1, keepdims=True)
    qn = jnp.max(jnp.where(lane == KQ_NORM_LANE, best, 0.0), axis=1, keepdims=True)
    safe = 2.0 * jnp.sqrt(jnp.max(kn * qn)) <= SAFE_EXP2_RANGE

    head = pl.program_id(1)
    mine = (lane >= K_PART_BASE) & (lane < K_PART_BASE + C_PARTS * N_HEADS) & (lane % N_HEADS == head)

    def neg_c(row):
        r = k_ref[0, 0, row:row + 1, HEAD_DIM:QK_WIDTH].astype(F32)
        return jnp.sum(jnp.where(mine, r, 0.0), axis=1, keepdims=True)

    n_q = k_ref.shape[2] // tq
    tile_end = [neg_c((j + 1) * tk - 1) for j in range(2 * (n_q - 1))]
    for i in range(n_q):
        q_start = neg_c(i * tq)
        dead = [(q_start - tile_end[j] > SKIP_DECAY_LOG2).astype(jnp.int32) for j in range(2 * i)]
        first_ref[i] = sum(dead)[0, 0] if dead else 0

    def query_tile(i, _):
        qt = qt_ref.at[0, 0, i]
        qn_row = qt[HEAD_DIM + Q_NORM_LANE:HEAD_DIM + Q_NORM_LANE + 1, :].astype(F32)
        acc_ref = acc_all.at[i]
        l_ref = l_all.at[i]
        acc_ref[...] = jnp.zeros_like(acc_ref)
        d = 2 * i

        @pl.when(safe)
        def _():
            shift = jnp.sqrt(qn_row * kn) + 1.0

            def tiles(j, n, l):
                kb = k_ref[0, 0, pl.ds(pl.multiple_of(j * tk, tk), n * tk), :]
                p = jnp.exp2(_dot(kb, qt[...]) - shift)
                vt = jnp.concatenate([vt_ref[0, 0, j + r] for r in range(n)], axis=1)
                acc_ref[...] += _dot(vt, p.astype(BF16))
                return l + jnp.sum(p, axis=0, keepdims=True)

            first = first_ref[i]
            l_ref[0:1, :] = jnp.zeros((1, tq), F32)

            @pl.when((first & 1) == 1)
            def _():
                l_ref[0:1, :] = tiles(first, 1, l_ref[0:1, :])

            pair0 = first + (first & 1)
            odd_pairs = lax.shift_right_logical(d - pair0, 1) & 1

            @pl.when(odd_pairs == 1)
            def _():
                l_ref[0:1, :] = tiles(pair0, 2, l_ref[0:1, :])

            quad0 = pair0 + 2 * odd_pairs
            l = lax.fori_loop(0, lax.shift_right_logical(d - quad0, 2),
                              lambda t, l: tiles(quad0 + 4 * t, 4, l), l_ref[0:1, :])
            s_lo = _dot(kv_rows(d), qt[...])
            s_hi = _dot(kv_rows(d + 1), qt[:, half:])
            p = jnp.exp2(causal(s_lo) - shift)
            p_hi = jnp.exp2(causal(s_hi) - shift[:, half:])
            acc_ref[...] += _dot(vt_ref[0, 0, d], p.astype(BF16))
            acc_ref[:, half:] += _dot(vt_ref[0, 0, d + 1], p_hi.astype(BF16))
            l = l + jnp.sum(p, axis=0, keepdims=True)
            l_ref[0:1, :] = jnp.concatenate(
                [l[:, :half], l[:, half:] + jnp.sum(p_hi, axis=0, keepdims=True)], axis=1)

        @pl.when(jnp.logical_not(safe))
        def _():
            def online(s, j, carry, cols):
                m, l = carry
                m_new = jnp.maximum(m, jnp.max(s, axis=0, keepdims=True))
                alpha = jnp.exp2(m - m_new)
                p = jnp.exp2(s - m_new)
                acc_ref[:, cols] = alpha * acc_ref[:, cols] + _dot(vt_ref[0, 0, j], p.astype(BF16))
                return m_new, alpha * l + jnp.sum(p, axis=0, keepdims=True)

            every = slice(0, tq)
            init = (jnp.full((1, tq), MASK_VALUE, F32), jnp.zeros((1, tq), F32))
            carry = lax.fori_loop(
                0, d, lambda j, c: online(_dot(kv_rows(j), qt[...]), j, c, every), init)
            m, l = online(causal(_dot(kv_rows(d), qt[...])), d, carry, every)
            upper = slice(half, tq)
            _, l_hi = online(causal(_dot(kv_rows(d + 1), qt[:, half:])), d + 1,
                             (m[:, half:], l[:, half:]), upper)
            l_ref[0:1, :] = jnp.concatenate([l[:, :half], l_hi], axis=1)

        return 0

    lax.fori_loop(0, n_q, query_tile, 0)
    for i in range(n_q):
        o = (acc_all[i] / l_all[i, 0:1, :]).T
        rows = slice(i * tq, (i + 1) * tq)
        o_ref[rows, :] = (ga_ref[rows, :].astype(F32) * o + gob_ref[rows, :].astype(F32)).astype(BF16)


def _attn(qt, k, vt, ga, gob):
    batch, n_heads, seq, _ = k.shape
    tq = Q_TILE
    head_cols = lambda b, h: (b, h)
    whole = lambda b, h: (b, h, 0, 0, 0)
    return pl.pallas_call(
        _attn_kernel,
        grid=(batch, n_heads),
        scratch_shapes=[pltpu.VMEM((seq // tq, HEAD_DIM, tq), F32),
                        pltpu.VMEM((seq // tq, 8, tq), F32),
                        pltpu.SMEM((seq // tq,), jnp.int32)],
        in_specs=[
            pl.BlockSpec((1, 1, seq // tq, QK_WIDTH, tq), whole),
            pl.BlockSpec((1, 1, seq, QK_WIDTH), lambda b, h: (b, h, 0, 0)),
            pl.BlockSpec((1, 1, seq // KV_TILE, HEAD_DIM, KV_TILE), whole),
            pl.BlockSpec((seq, HEAD_DIM), head_cols),
            pl.BlockSpec((seq, HEAD_DIM), head_cols),
        ],
        out_specs=pl.BlockSpec((seq, HEAD_DIM), head_cols),
        out_shape=jax.ShapeDtypeStruct(ga.shape, BF16),
        compiler_params=pltpu.CompilerParams(
            dimension_semantics=("parallel", "parallel"), vmem_limit_bytes=VMEM_LIMIT_BYTES),
        name="attn",
    )(qt, k, vt, ga, gob)


def _pack_w_in(w_in):
    d = w_in.shape[0]
    col_f = 3 * D_MODEL
    first_col = {SEC_Q: 0, SEC_K: D_MODEL, SEC_V: 2 * D_MODEL, SEC_U: col_f + N_HEADS,
                 SEC_SV: col_f + N_HEADS + D_MODEL, SEC_GA: col_f + N_HEADS + 2 * D_MODEL,
                 SEC_GB: col_f + N_HEADS + 3 * D_MODEL}
    starts = [first_col[sec] for sec in range(N_SECTIONS)]
    n_c = N_HEADS // HEADS_PER_STEP
    secs = [w_in[:, s:s + D_MODEL].reshape(d, n_c, 1, HEADS_PER_STEP * HEAD_DIM) for s in starts]
    wmain = jnp.concatenate(secs, axis=2).reshape(d, n_c * CHUNK_COLS).astype(BF16)
    wf = w_in[:, col_f:col_f + N_HEADS]
    return wmain, _replicate_forget(wf).astype(BF16)


def _replicate_forget(a):
    rows = a.shape[0]
    rep = jnp.tile(a, (1, C_PARTS))
    gap = jnp.zeros((rows, K_PART_BASE - C_PARTS * N_HEADS), a.dtype)
    tail = jnp.zeros((rows, LANES - K_PART_BASE - C_PARTS * N_HEADS), a.dtype)
    return jnp.concatenate([rep, gap, rep, tail], axis=1)


def kernel(x, ffn1_pre_g, ffn1_w_gate, ffn1_w_up, ffn1_w_down, ffn1_post_g, mix_pre_g, w_in, b_forget, sgu_ln_g, sgu_ln_b, sgu_w_s, sgu_b_s, w_out, mix_post_g, ffn2_pre_g, ffn2_w_gate, ffn2_w_up, ffn2_w_down, ffn2_post_g):
    batch, seq, d = x.shape
    depth = ffn1_pre_g.shape[0]
    tm_ffn = min(FFN_ROWS, seq)
    tm_proj = min(PROJ_ROWS, seq)
    xf = x.reshape(batch * seq, d)
    for l in range(depth):
        row = lambda a: a[l].reshape(1, -1)
        xf, h = _ffn(xf, row(ffn1_pre_g), ffn1_w_gate[l].astype(BF16), ffn1_w_up[l].astype(BF16),
                     ffn1_w_down[l].astype(BF16), row(ffn1_post_g), next_g=row(mix_pre_g),
                     tm=min(FFN_ROWS_PLAIN, seq), tf=FFN_CHUNK)
        wmain, wf = _pack_w_in(w_in[l])
        qt, k, vt, ga, gob = _proj(
            h, wmain, wf, _replicate_forget(row(b_forget)),
            row(sgu_ln_g), row(sgu_ln_b), sgu_w_s[l], sgu_b_s[l].reshape(N_HEADS, SGU_LEN, 1),
            batch=batch, seq=seq, tm=tm_proj)
        merged = _attn(qt, k, vt, ga, gob)
        xf = _ffn(xf, row(ffn2_pre_g), ffn2_w_gate[l].astype(BF16), ffn2_w_up[l].astype(BF16),
                  ffn2_w_down[l].astype(BF16), row(ffn2_post_g),
                  mix=(merged, w_out[l].astype(BF16), row(mix_post_g)), tm=tm_ffn, tf=FFN_CHUNK)
    return xf.reshape(batch, seq, d)
```
